```python
import jax, jax.numpy as jnp
from jax import lax
import numpy as np

D_MODEL = 1024
BATCH = 32
SEQ = 2048
DEPTH = 4

N_A = DEPTH // 2
N_B = DEPTH - N_A
N_DENSE = (DEPTH + 1) // 2
N_MOE = DEPTH // 2

POOL_WINDOWS = (2, 4, 8, 16)
N_POOL_GROUPS = len(POOL_WINDOWS)
GW = D_MODEL // N_POOL_GROUPS

HEAD_DIM = 64
N_HEADS = D_MODEL // HEAD_DIM
Q_BLOCK = 128

D_FF = 2816
N_EXPERTS = 8
TOP_K = 2
D_FF_EXPERT = 3584
EXPERT_BLOCK = 128

RMS_EPS = 1e-6

kernel_name = "yoco_pool_stickbreak_moe_trunk"


def rmsnorm(x, g):
    xf = x.astype(jnp.float32)
    inv = lax.rsqrt(jnp.mean(xf * xf, axis=-1, keepdims=True) + RMS_EPS)
    return (xf * inv).astype(x.dtype) * g


def multiscale_pool(h, w_groups, scale):
    B, S, D = h.shape
    hf = h.astype(jnp.float32)
    cs = jnp.cumsum(hf, axis=1)
    t = jnp.arange(S)
    pooled = []
    for gi, w in enumerate(POOL_WINDOWS):
        sl = slice(gi * GW, (gi + 1) * GW)
        csg = cs[..., sl]
        lagged = jnp.pad(csg, ((0, 0), (w, 0), (0, 0)))[:, :S]
        count = jnp.minimum(t + 1, w).astype(jnp.float32)[None, :, None]
        pooled.append((csg - lagged) / count - hf[..., sl])
    p = jnp.stack(pooled, axis=2).astype(h.dtype)
    y = jnp.einsum('bsgc,gcd->bsgd', p, w_groups).reshape(B, S, D)
    return y * scale


def stick_breaking_attention(q, k, v):
    S = q.shape[2]
    scale = HEAD_DIM ** -0.5
    outs = []
    for i in range(S // Q_BLOCK):
        L = (i + 1) * Q_BLOCK
        qb = q[:, :, i * Q_BLOCK:L]
        kb = k[:, :, :L]
        vb = v[:, :, :L]
        z = jnp.einsum('bhqd,bhkd->bhqk', qb, kb).astype(jnp.float32) * scale
        qpos = i * Q_BLOCK + jnp.arange(Q_BLOCK)
        kpos = jnp.arange(L)
        causal = kpos[None, :] < qpos[:, None]
        log_beta = jax.nn.log_sigmoid(z)
        log_keep = jnp.where(causal, log_beta - z, 0.0)
        log_after = lax.cumsum(log_keep, axis=3, reverse=True) - log_keep
        a = jnp.where(causal, jnp.exp(log_beta + log_after), 0.0)
        outs.append(jnp.einsum('bhqk,bhkd->bhqd', a.astype(vb.dtype), vb))
    return jnp.concatenate(outs, axis=2)


def split_heads(t):
    B, S, _ = t.shape
    return t.reshape(B, S, N_HEADS, HEAD_DIM).transpose(0, 2, 1, 3)


def merge_heads(t):
    B, H, S, Dh = t.shape
    return t.transpose(0, 2, 1, 3).reshape(B, S, H * Dh)


def swiglu(h, w_in, w_out):
    g, u = jnp.split(h @ w_in, 2, axis=-1)
    return (jax.nn.silu(g) * u) @ w_out


def moe_swiglu(h, w_router, w_in, w_out):
    B, S, D = h.shape
    N = B * S
    NK = N * TOP_K
    hf = h.reshape(N, D)
    logits = hf.astype(jnp.float32) @ w_router.astype(jnp.float32)
    top_v, top_e = lax.top_k(logits, TOP_K)
    gates = jax.nn.softmax(top_v, axis=-1)
    flat_e = top_e.reshape(NK).astype(jnp.int32)
    flat_g = gates.reshape(NK).astype(h.dtype)
    flat_tok = jnp.arange(NK, dtype=jnp.int32) // TOP_K

    order = jnp.argsort(flat_e)
    sorted_e = flat_e[order]
    counts = jnp.bincount(flat_e, length=N_EXPERTS).astype(jnp.int32)
    padded = (counts + EXPERT_BLOCK - 1) // EXPERT_BLOCK * EXPERT_BLOCK
    pend = jnp.cumsum(padded)
    pstart = pend - padded
    cstart = jnp.cumsum(counts) - counts
    dest = pstart[sorted_e] + jnp.arange(NK, dtype=jnp.int32) - cstart[sorted_e]

    cap = NK + N_EXPERTS * EXPERT_BLOCK
    n_blk = cap // EXPERT_BLOCK
    tok_buf = jnp.zeros((cap,), jnp.int32).at[dest].set(flat_tok[order])
    gate_buf = jnp.zeros((cap,), h.dtype).at[dest].set(flat_g[order])
    blk_start = jnp.arange(n_blk, dtype=jnp.int32) * EXPERT_BLOCK
    blk_e = jnp.minimum(jnp.searchsorted(pend, blk_start, side='right'), N_EXPERTS - 1)

    xs = hf[tok_buf].reshape(n_blk, EXPERT_BLOCK, D)

    def expert_block(args):
        xb, e = args
        g, u = jnp.split(xb @ w_in[e], 2, axis=-1)
        return (jax.nn.silu(g) * u) @ w_out[e]

    ys = lax.map(expert_block, (xs, blk_e)).reshape(cap, D)
    y = jnp.zeros_like(hf).at[tok_buf].add(ys * gate_buf[:, None])
    return y.reshape(B, S, D)


def setup_inputs(seed: int = 0) -> dict:
    key = jax.random.key(seed)
    ks = jax.random.split(key, 16)
    f32 = jnp.float32

    def nrm(k, shape, fan_in):
        return jax.random.normal(k, shape, f32) * (fan_in ** -0.5)

    def gain(k, shape):
        return 1.0 + 0.02 * jax.random.normal(k, shape, f32)

    return {
        "x": jax.random.normal(ks[0], (BATCH, SEQ, D_MODEL), f32),
        "g_mix": gain(ks[1], (DEPTH, D_MODEL)),
        "g_ffn": gain(ks[2], (DEPTH, D_MODEL)),
        "pool_w": nrm(ks[3], (N_A, N_POOL_GROUPS, GW, GW), GW),
        "pool_scale": gain(ks[4], (N_A, D_MODEL)),
        "g_kv": gain(ks[5], (D_MODEL,)),
        "w_kv": nrm(ks[6], (D_MODEL, 2 * D_MODEL), D_MODEL),
        "w_q": nrm(ks[7], (N_B, D_MODEL, D_MODEL), D_MODEL),
        "w_o": nrm(ks[8], (N_B, D_MODEL, D_MODEL), D_MODEL),
        "ffn_in": nrm(ks[9], (N_DENSE, D_MODEL, 2 * D_FF), D_MODEL),
        "ffn_out": nrm(ks[10], (N_DENSE, D_FF, D_MODEL), D_FF),
        "w_router": nrm(ks[11], (N_MOE, D_MODEL, N_EXPERTS), D_MODEL),
        "moe_in": nrm(ks[12], (N_MOE, N_EXPERTS, D_MODEL, 2 * D_FF_EXPERT), D_MODEL),
        "moe_out": nrm(ks[13], (N_MOE, N_EXPERTS, D_FF_EXPERT, D_MODEL), D_FF_EXPERT),
        "g_final": gain(ks[14], (D_MODEL,)),
    }


def reference(x, g_mix, g_ffn, pool_w, pool_scale, g_kv, w_kv, w_q, w_o,
              ffn_in, ffn_out, w_router, moe_in, moe_out, g_final):
    k_sh = None
    v_sh = None
    for l in range(DEPTH):
        h = rmsnorm(x, g_mix[l])
        if l < N_A:
            x = x + multiscale_pool(h, pool_w[l], pool_scale[l])
        else:
            b = l - N_A
            q = split_heads(h @ w_q[b])
            o = stick_breaking_attention(q, k_sh, v_sh)
            x = x + merge_heads(o) @ w_o[b]
        h = rmsnorm(x, g_ffn[l])
        if l % 2 == 0:
            x = x + swiglu(h, ffn_in[l // 2], ffn_out[l // 2])
        else:
            x = x + moe_swiglu(h, w_router[l // 2], moe_in[l // 2], moe_out[l // 2])
        if l == N_A - 1:
            kv = rmsnorm(x, g_kv) @ w_kv
            k_part, v_part = jnp.split(kv, 2, axis=-1)
            k_sh = split_heads(k_part)
            v_sh = split_heads(v_part)
    return rmsnorm(x, g_final)
```

```python
import functools

import jax
import jax.numpy as jnp
from jax import lax
from jax.experimental import pallas as pl
from jax.experimental.pallas import tpu as pltpu

RMS_EPS = 1e-6
POOL_WINDOWS = (2, 4, 8, 16)
POOL_HALO = 16
HEAD_DIM = 64
TOP_K = 2
LANES = 128
SUBLANES = 8
KEY_BLOCK = 128
KEY_CHUNK = KEY_BLOCK // SUBLANES
VMEM_LIMIT = 56 * 1024 * 1024

F32 = jnp.float32
BF16 = jnp.bfloat16


def _cparams(sem):
    return pltpu.CompilerParams(dimension_semantics=sem, vmem_limit_bytes=VMEM_LIMIT)


def _rmsnorm(x, g):
    inv = lax.rsqrt(jnp.mean(x * x, axis=-1, keepdims=True) + RMS_EPS)
    return (x * inv) * g


def _dot(a, b):
    return jnp.dot(a, b, preferred_element_type=F32)


def _silu(g):
    return g * (1.0 / (1.0 + jnp.exp(-g)))


def _pool_kernel(x_ref, g_ref, w_ref, sc_ref, o_ref, l0, l1, l2, l3, *, ts, gw):
    si = pl.program_id(1)
    levels = (l0, l1, l2, l3)
    d = x_ref.shape[-1]

    @pl.when(si == 0)
    def _():
        for lv in levels:
            lv[0:POOL_HALO, :] = jnp.zeros((POOL_HALO, d), F32)

    x = x_ref[0]
    h = _rmsnorm(x, g_ref[...])
    l0[POOL_HALO:POOL_HALO + ts, :] = h
    for k in range(3):
        sh = 1 << k
        c0 = k * gw
        src, dst = levels[k], levels[k + 1]
        dst[POOL_HALO:POOL_HALO + ts, c0:] = (
            src[POOL_HALO:POOL_HALO + ts, c0:] + src[POOL_HALO - sh:POOL_HALO - sh + ts, c0:])
    c3 = 3 * gw
    s16 = l3[POOL_HALO:POOL_HALO + ts, c3:] + l3[POOL_HALO - 8:POOL_HALO - 8 + ts, c3:]

    t = si * ts + lax.broadcasted_iota(jnp.int32, (ts, 1), 0)
    for gi, w in enumerate(POOL_WINDOWS):
        sl = slice(gi * gw, (gi + 1) * gw)
        if gi < 3:
            s = levels[gi + 1][POOL_HALO:POOL_HALO + ts, sl]
        else:
            s = s16
        cnt = jnp.minimum(t + 1, w).astype(F32)
        p = (s / cnt - h[:, sl]).astype(BF16)
        y = _dot(p, w_ref[gi])
        o_ref[0, :, sl] = x[:, sl] + y * sc_ref[:, sl]

    for lv in levels:
        lv[0:POOL_HALO, :] = lv[ts:ts + POOL_HALO, :]


def _pool_layer(x, g, w_bf, scale, *, ts):
    b, s, d = x.shape
    ng, gw, _ = w_bf.shape
    ts = min(ts, s)
    kern = functools.partial(_pool_kernel, ts=ts, gw=gw)
    return pl.pallas_call(
        kern,
        out_shape=jax.ShapeDtypeStruct((b, s, d), F32),
        grid=(b, s // ts),
        in_specs=[
            pl.BlockSpec((1, ts, d), lambda bi, si: (bi, si, 0)),
            pl.BlockSpec((1, d), lambda bi, si: (0, 0)),
            pl.BlockSpec((ng, gw, gw), lambda bi, si: (0, 0, 0)),
            pl.BlockSpec((1, d), lambda bi, si: (0, 0)),
        ],
        out_specs=pl.BlockSpec((1, ts, d), lambda bi, si: (bi, si, 0)),
        scratch_shapes=[pltpu.VMEM((ts + POOL_HALO, d), F32) for _ in range(4)],
        compiler_params=_cparams(("arbitrary", "arbitrary")),
        name="pool_layer",
    )(x, g.reshape(1, d), w_bf, scale.reshape(1, d))


def _ffn_kernel(x_ref, g_ref, wg_ref, wu_ref, wo_ref, o_ref, h_sc):
    j = pl.program_id(1)

    @pl.when(j == 0)
    def _():
        h_sc[...] = _rmsnorm(x_ref[...], g_ref[...]).astype(BF16)

    h = h_sc[...]
    a = _silu(_dot(h, wg_ref[...])) * _dot(h, wu_ref[...])
    part = _dot(a.astype(BF16), wo_ref[...])

    @pl.when(j == 0)
    def _():
        o_ref[...] = x_ref[...] + part

    @pl.when(j > 0)
    def _():
        o_ref[...] += part


def _dense_ffn(x2, g, w_in_bf, w_out_bf, *, tm, tf):
    n, d = x2.shape
    f = w_out_bf.shape[0]
    tm = min(tm, n)
    tf = min(tf, f)
    nj = f // tf
    return pl.pallas_call(
        _ffn_kernel,
        out_shape=jax.ShapeDtypeStruct((n, d), F32),
        grid=(n // tm, nj),
        in_specs=[
            pl.BlockSpec((tm, d), lambda i, j: (i, 0)),
            pl.BlockSpec((1, d), lambda i, j: (0, 0)),
            pl.BlockSpec((d, tf), lambda i, j: (0, j)),
            pl.BlockSpec((d, tf), lambda i, j: (0, j + nj)),
            pl.BlockSpec((tf, d), lambda i, j: (j, 0)),
        ],
        out_specs=pl.BlockSpec((tm, d), lambda i, j: (i, 0)),
        scratch_shapes=[pltpu.VMEM((tm, d), BF16)],
        compiler_params=_cparams(("arbitrary", "arbitrary")),
        name="dense_ffn",
    )(x2, g.reshape(1, d), w_in_bf, w_in_bf, w_out_bf)


def _proj_kernel(x_ref, g_ref, w_ref, o_ref, *, out_scale):
    h = _rmsnorm(x_ref[...], g_ref[...]).astype(BF16)
    y = _dot(h, w_ref[...])
    if out_scale != 1.0:
        y = y * out_scale
    o_ref[...] = y.astype(BF16)


def _norm_proj(x2, g, w_bf, *, tm, out_scale=1.0):
    n, d = x2.shape
    dn = w_bf.shape[1]
    tm = min(tm, n)
    return pl.pallas_call(
        functools.partial(_proj_kernel, out_scale=out_scale),
        out_shape=jax.ShapeDtypeStruct((n, dn), BF16),
        grid=(n // tm,),
        in_specs=[
            pl.BlockSpec((tm, d), lambda i: (i, 0)),
            pl.BlockSpec((1, d), lambda i: (0, 0)),
            pl.BlockSpec((d, dn), lambda i: (0, 0)),
        ],
        out_specs=pl.BlockSpec((tm, dn), lambda i: (i, 0)),
        compiler_params=_cparams(("arbitrary",)),
        name="norm_proj",
    )(x2, g.reshape(1, d), w_bf)


def _oproj_kernel(x_ref, o_in_ref, w_ref, o_ref):
    o_ref[...] = x_ref[...] + _dot(o_in_ref[...], w_ref[...])


def _out_proj(x2, o_bf, w_bf, *, tm):
    n, d = x2.shape
    tm = min(tm, n)
    return pl.pallas_call(
        _oproj_kernel,
        out_shape=jax.ShapeDtypeStruct((n, d), F32),
        grid=(n // tm,),
        in_specs=[
            pl.BlockSpec((tm, d), lambda i: (i, 0)),
            pl.BlockSpec((tm, d), lambda i: (i, 0)),
            pl.BlockSpec((d, d), lambda i: (0, 0)),
        ],
        out_specs=pl.BlockSpec((tm, d), lambda i: (i, 0)),
        compiler_params=_cparams(("arbitrary",)),
        name="out_proj",
    )(x2, o_bf, w_bf)


def _attn_kernel(qt_ref, kp_ref, vt_ref, o_ref, *, tq):
    qi = pl.program_id(2)
    kb = KEY_BLOCK
    nr = kb // SUBLANES
    qt = qt_ref[0, 0]
    q_pos = qi * tq + lax.broadcasted_iota(jnp.int32, (kb, tq), 1)
    row = lax.broadcasted_iota(jnp.int32, (kb, tq), 0)
    key_off = KEY_CHUNK * (row % SUBLANES) + row // SUBLANES
    sub = lax.broadcasted_iota(jnp.int32, (SUBLANES, tq), 0)

    def block(j, carry, masked):
        c, acc = carry
        z = _dot(kp_ref[0, 0, j], qt)
        sp = jnp.maximum(z, 0.0) + jnp.log(1.0 + jnp.exp(-jnp.abs(z)))
        if masked:
            m = (j * kb + key_off) < q_pos
            sp = jnp.where(m, sp, 0.0)
        run = jnp.zeros((SUBLANES, tq), F32)
        parts = [None] * nr
        for r in reversed(range(nr)):
            parts[r] = run
            run = run + sp[r * SUBLANES:(r + 1) * SUBLANES, :]
        y = jnp.where(sub + 1 < SUBLANES, pltpu.roll(run, SUBLANES - 1, axis=0), 0.0)
        for dd in (1, 2, 4):
            y = y + jnp.where(sub + dd < SUBLANES, pltpu.roll(y, SUBLANES - dd, axis=0), 0.0)
        oc = y + c
        tot = y[0:1, :] + run[0:1, :]
        rows = []
        for r in range(nr):
            sl = slice(r * SUBLANES, (r + 1) * SUBLANES)
            a = jnp.exp((z[sl, :] - sp[sl, :]) - (parts[r] + oc))
            if masked:
                a = jnp.where(m[sl, :], a, 0.0)
            rows.append(a)
        at = jnp.concatenate(rows, axis=0).astype(BF16)
        acc = acc + _dot(vt_ref[0, 0, j], at)
        return c + tot, acc

    carry = (jnp.zeros((1, tq), F32), jnp.zeros((HEAD_DIM, tq), F32))
    nd = tq // kb
    for dj in reversed(range(nd)):
        carry = block(qi * nd + dj, carry, True)

    def body(it, carry):
        return block(qi * nd - 1 - it, carry, False)

    carry = lax.fori_loop(0, qi * nd, body, carry)
    o_ref[0, 0] = carry[1].astype(BF16)


def _attention(qt, kp, vt, *, tq):
    b, nh, hd, s = qt.shape
    nkb = kp.shape[2]
    tq = min(tq, s)
    return pl.pallas_call(
        functools.partial(_attn_kernel, tq=tq),
        out_shape=jax.ShapeDtypeStruct((b, nh, hd, s), BF16),
        grid=(b, nh, s // tq),
        in_specs=[
            pl.BlockSpec((1, 1, hd, tq), lambda bi, hi, qi: (bi, hi, 0, qi)),
            pl.BlockSpec((1, 1, nkb, KEY_BLOCK, hd), lambda bi, hi, qi: (bi, hi, 0, 0, 0)),
            pl.BlockSpec((1, 1, nkb, hd, KEY_BLOCK), lambda bi, hi, qi: (bi, hi, 0, 0, 0)),
        ],
        out_specs=pl.BlockSpec((1, 1, hd, tq), lambda bi, hi, qi: (bi, hi, 0, qi)),
        compiler_params=_cparams(("arbitrary", "arbitrary", "arbitrary")),
        name="stickbreak_attn",
    )(qt, kp, vt)


def _split_heads_kv(kv_bf, b, s):
    d = kv_bf.shape[1] // 2
    nh = d // HEAD_DIM
    nkb = s // KEY_BLOCK

    def heads(t):
        t = t.reshape(b, nkb, SUBLANES, KEY_CHUNK, nh, HEAD_DIM)
        return t.transpose(0, 4, 1, 3, 2, 5).reshape(b, nh, nkb, KEY_BLOCK, HEAD_DIM)

    kp = heads(kv_bf[:, :d])
    vt = heads(kv_bf[:, d:]).transpose(0, 1, 2, 4, 3)
    return kp, vt


def _router_kernel(x_ref, g_ref, wr_ref, h_ref, meta_ref, gate_ref, cnt_ref, run_sc, *, n_exp):
    i = pl.program_id(0)
    tm = x_ref.shape[0]

    @pl.when(i == 0)
    def _():
        run_sc[...] = jnp.zeros_like(run_sc)

    h = _rmsnorm(x_ref[...], g_ref[...])
    h_ref[...] = h
    logits = jnp.dot(h, wr_ref[...], preferred_element_type=F32, precision=lax.Precision.HIGHEST)
    lane = lax.broadcasted_iota(jnp.int32, (tm, LANES), 1)
    neg = jnp.float32(-jnp.inf)
    logits = jnp.where(lane < n_exp, logits, neg)
    m1 = jnp.max(logits, axis=-1, keepdims=True)
    i1 = jnp.min(jnp.where(logits == m1, lane, LANES), axis=-1, keepdims=True)
    rest = jnp.where(lane == i1, neg, logits)
    m2 = jnp.max(rest, axis=-1, keepdims=True)
    i2 = jnp.min(jnp.where(rest == m2, lane, LANES), axis=-1, keepdims=True)
    e = jnp.exp(m2 - m1)
    den = 1.0 + e
    g1 = 1.0 / den
    g2 = e / den

    sel1 = lane == i1
    sel2 = lane == i2
    onehot = jnp.where(sel1 | sel2, 1.0, 0.0)
    tri = (lax.broadcasted_iota(jnp.int32, (tm, tm), 1)
           < lax.broadcasted_iota(jnp.int32, (tm, tm), 0)).astype(BF16)
    before = _dot(tri, onehot.astype(BF16)) + run_sc[...]
    r1 = jnp.sum(jnp.where(sel1, before, 0.0), axis=-1, keepdims=True).astype(jnp.int32)
    r2 = jnp.sum(jnp.where(sel2, before, 0.0), axis=-1, keepdims=True).astype(jnp.int32)
    run_sc[...] += jnp.sum(onehot, axis=0, keepdims=True)

    meta = jnp.where(lane == 0, i1, jnp.where(lane == 1, i2, jnp.where(lane == 2, r1, jnp.where(lane == 3, r2, 0))))
    gates = jnp.where(lane == 0, g1, jnp.where(lane == 1, g2, 0.0))
    meta_ref[...] = meta[:, :SUBLANES]
    gate_ref[...] = gates[:, :SUBLANES]
    cnt_ref[...] = run_sc[...].astype(jnp.int32)


def _router(x2, g, w_router, *, tm):
    n, d = x2.shape
    n_exp = w_router.shape[1]
    tm = min(tm, n)
    wr = jnp.zeros((d, LANES), F32).at[:, :n_exp].set(w_router)
    return pl.pallas_call(
        functools.partial(_router_kernel, n_exp=n_exp),
        out_shape=(
            jax.ShapeDtypeStruct((n, d), F32),
            jax.ShapeDtypeStruct((n, SUBLANES), jnp.int32),
            jax.ShapeDtypeStruct((n, SUBLANES), F32),
            jax.ShapeDtypeStruct((1, LANES), jnp.int32),
        ),
        grid=(n // tm,),
        in_specs=[
            pl.BlockSpec((tm, d), lambda i: (i, 0)),
            pl.BlockSpec((1, d), lambda i: (0, 0)),
            pl.BlockSpec((d, LANES), lambda i: (0, 0)),
        ],
        out_specs=(
            pl.BlockSpec((tm, d), lambda i: (i, 0)),
            pl.BlockSpec((tm, SUBLANES), lambda i: (i, 0)),
            pl.BlockSpec((tm, SUBLANES), lambda i: (i, 0)),
            pl.BlockSpec((1, LANES), lambda i: (0, 0)),
        ),
        scratch_shapes=[pltpu.VMEM((1, LANES), F32)],
        compiler_params=_cparams(("arbitrary",)),
        name="moe_router",
    )(x2, g.reshape(1, d), wr)


def _row_copy(src, si, dst, di, sem):
    return pltpu.make_async_copy(src.at[pl.ds(si, 1), :], dst.at[pl.ds(di, 1), :], sem)


def _dispatch_kernel(dest_ref, h_ref, xs_in_ref, xs_ref, sem):
    del xs_in_ref
    tm = h_ref.shape[0]

    def issue(r, carry):
        for k in range(TOP_K):
            _row_copy(h_ref, r, xs_ref, dest_ref[TOP_K * r + k], sem).start()
        return carry

    lax.fori_loop(0, tm, issue, 0)

    def drain(r, carry):
        for k in range(TOP_K):
            _row_copy(h_ref, r, xs_ref, dest_ref[TOP_K * r + k], sem).wait()
        return carry

    lax.fori_loop(0, tm, drain, 0)


def _dispatch(h2, dest, cap, *, tm):
    n, d = h2.shape
    tm = min(tm, n)
    xs0 = jnp.zeros((cap, d), F32)
    return pl.pallas_call(
        _dispatch_kernel,
        out_shape=jax.ShapeDtypeStruct((cap, d), F32),
        grid=(n // tm,),
        in_specs=[
            pl.BlockSpec((TOP_K * tm,), lambda i: (i,), memory_space=pltpu.SMEM),
            pl.BlockSpec((tm, d), lambda i: (i, 0)),
            pl.BlockSpec(memory_space=pl.ANY),
        ],
        out_specs=pl.BlockSpec(memory_space=pl.ANY),
        scratch_shapes=[pltpu.SemaphoreType.DMA],
        input_output_aliases={2: 0},
        compiler_params=_cparams(("arbitrary",)),
        name="moe_dispatch",
    )(dest, h2, xs0)


def _expert_kernel(te_ref, nu_ref, x_ref, wg_ref, wu_ref, wo_ref, y_ref, xb_sc):
    i = pl.program_id(0)
    j = pl.program_id(1)
    used = i < nu_ref[0]

    @pl.when(used & (j == 0))
    def _():
        xb_sc[...] = x_ref[...].astype(BF16)

    @pl.when(used)
    def _():
        xb = xb_sc[...]
        a = _silu(_dot(xb, wg_ref[0])) * _dot(xb, wu_ref[0])
        part = _dot(a.astype(BF16), wo_ref[0])

        @pl.when(j == 0)
        def _():
            y_ref[...] = part

        @pl.when(j > 0)
        def _():
            y_ref[...] += part

    @pl.when(jnp.logical_not(used) & (j == 0))
    def _():
        y_ref[...] = jnp.zeros_like(y_ref)


def _expert_ffn(xs, tile_e, n_used, w_in_bf, w_out_bf, *, tm, tf):
    cap, d = xs.shape
    n_exp, f, _ = w_out_bf.shape
    tf = min(tf, f)
    nj = f // tf
    n_tiles = cap // tm

    def jeff(i, j, nu):
        return jnp.where(i < nu[0], j, nj - 1)

    return pl.pallas_call(
        _expert_kernel,
        out_shape=jax.ShapeDtypeStruct((cap, d), F32),
        grid_spec=pltpu.PrefetchScalarGridSpec(
            num_scalar_prefetch=2,
            grid=(n_tiles, nj),
            in_specs=[
                pl.BlockSpec((tm, d), lambda i, j, te, nu: (jnp.minimum(i, nu[0] - 1), 0)),
                pl.BlockSpec((1, d, tf), lambda i, j, te, nu: (te[i], 0, jeff(i, j, nu))),
                pl.BlockSpec((1, d, tf), lambda i, j, te, nu: (te[i], 0, jeff(i, j, nu) + nj)),
                pl.BlockSpec((1, tf, d), lambda i, j, te, nu: (te[i], jeff(i, j, nu), 0)),
            ],
            out_specs=pl.BlockSpec((tm, d), lambda i, j, te, nu: (i, 0)),
            scratch_shapes=[pltpu.VMEM((tm, d), BF16)],
        ),
        compiler_params=_cparams(("arbitrary", "arbitrary")),
        name="moe_expert_ffn",
    )(tile_e, n_used, xs, w_in_bf, w_in_bf, w_out_bf)


def _combine_kernel(dest_ref, x_ref, gate_ref, gf_ref, ys_ref, o_ref, y_sc, sem, *, final_norm):
    tm = x_ref.shape[0]

    def issue(r, carry):
        for k in range(TOP_K):
            _row_copy(ys_ref, dest_ref[TOP_K * r + k], y_sc.at[k], r, sem).start()
        return carry

    lax.fori_loop(0, tm, issue, 0)

    def drain(r, carry):
        for k in range(TOP_K):
            _row_copy(ys_ref, dest_ref[TOP_K * r + k], y_sc.at[k], r, sem).wait()
        return carry

    lax.fori_loop(0, tm, drain, 0)

    g = gate_ref[...]
    out = x_ref[...] + (y_sc[0] * g[:, 0:1] + y_sc[1] * g[:, 1:2])
    if final_norm:
        out = _rmsnorm(out, gf_ref[...])
    o_ref[...] = out


def _combine(x2, gates, dest, ys, g_final, *, tm, final_norm):
    n, d = x2.shape
    tm = min(tm, n)
    return pl.pallas_call(
        functools.partial(_combine_kernel, final_norm=final_norm),
        out_shape=jax.ShapeDtypeStruct((n, d), F32),
        grid=(n // tm,),
        in_specs=[
            pl.BlockSpec((TOP_K * tm,), lambda i: (i,), memory_space=pltpu.SMEM),
            pl.BlockSpec((tm, d), lambda i: (i, 0)),
            pl.BlockSpec((tm, SUBLANES), lambda i: (i, 0)),
            pl.BlockSpec((1, d), lambda i: (0, 0)),
            pl.BlockSpec(memory_space=pl.ANY),
        ],
        out_specs=pl.BlockSpec((tm, d), lambda i: (i, 0)),
        scratch_shapes=[pltpu.VMEM((TOP_K, tm, d), F32), pltpu.SemaphoreType.DMA],
        compiler_params=_cparams(("arbitrary",)),
        name="moe_combine",
    )(dest, x2, gates, g_final.reshape(1, d), ys)


def _moe_layer(x2, g, w_router, w_in_bf, w_out_bf, g_final, *, final_norm, tm, tm_e, tf):
    n, d = x2.shape
    n_exp = w_router.shape[1]
    tm_e = min(tm_e, n)
    h, meta, gates, cnt = _router(x2, g, w_router, tm=tm)
    counts = cnt[0, :n_exp]
    padded = (counts + tm_e - 1) // tm_e * tm_e
    pend = jnp.cumsum(padded)
    pstart = pend - padded
    dest = (pstart[meta[:, 0:TOP_K]] + meta[:, TOP_K:2 * TOP_K]).reshape(-1).astype(jnp.int32)
    cap = n * TOP_K + n_exp * tm_e
    n_tiles = cap // tm_e
    n_used = (pend[-1] // tm_e).astype(jnp.int32)
    tile_start = jnp.arange(n_tiles, dtype=jnp.int32) * tm_e
    tile_e = jnp.minimum(jnp.searchsorted(pend, tile_start, side="right"), n_exp - 1).astype(jnp.int32)
    tile_e = jnp.where(jnp.arange(n_tiles) < n_used, tile_e, tile_e[n_used - 1])

    xs = _dispatch(h, dest, cap, tm=tm)
    ys = _expert_ffn(xs, tile_e, n_used.reshape(1), w_in_bf, w_out_bf, tm=tm_e, tf=tf)
    return _combine(x2, gates, dest, ys, g_final, tm=tm, final_norm=final_norm)


def _trunk(x, g_mix, g_ffn, pool_w, pool_scale, g_kv, w_kv, w_q, w_o, ffn_in, ffn_out,
           w_router, moe_in, moe_out, g_final, *, ts, tm, tf_dense, tm_e, tf_moe, tq):
    b, s, d = x.shape
    n = b * s
    depth = g_mix.shape[0]
    n_a = pool_w.shape[0]
    nh = d // HEAD_DIM
    kp = vt = None
    for l in range(depth):
        if l < n_a:
            x = _pool_layer(x, g_mix[l], pool_w[l].astype(BF16), pool_scale[l], ts=ts)
        else:
            bi = l - n_a
            x2 = x.reshape(n, d)
            q = _norm_proj(x2, g_mix[l], w_q[bi].astype(BF16), tm=tm, out_scale=HEAD_DIM ** -0.5)
            qt = q.reshape(b, s, nh, HEAD_DIM).transpose(0, 2, 3, 1)
            ot = _attention(qt, kp, vt, tq=tq)
            o = ot.transpose(0, 3, 1, 2).reshape(n, d)
            x = _out_proj(x2, o, w_o[bi].astype(BF16), tm=tm).reshape(b, s, d)
        x2 = x.reshape(n, d)
        if l % 2 == 0:
            x2 = _dense_ffn(x2, g_ffn[l], ffn_in[l // 2].astype(BF16), ffn_out[l // 2].astype(BF16),
                            tm=tm, tf=tf_dense)
        else:
            x2 = _moe_layer(x2, g_ffn[l], w_router[l // 2], moe_in[l // 2].astype(BF16),
                            moe_out[l // 2].astype(BF16), g_final,
                            final_norm=(l == depth - 1), tm=tm, tm_e=tm_e, tf=tf_moe)
        x = x2.reshape(b, s, d)
        if l == n_a - 1:
            kv = _norm_proj(x2, g_kv, w_kv.astype(BF16), tm=tm)
            kp, vt = _split_heads_kv(kv, b, s)
    return x


def kernel(x, g_mix, g_ffn, pool_w, pool_scale, g_kv, w_kv, w_q, w_o, ffn_in, ffn_out,
           w_router, moe_in, moe_out, g_final):
    return _trunk(x, g_mix, g_ffn, pool_w, pool_scale, g_kv, w_kv, w_q, w_o, ffn_in, ffn_out,
                  w_router, moe_in, moe_out, g_final,
                  ts=512, tm=512, tf_dense=1408, tm_e=512, tf_moe=896, tq=256)
```

```python
import functools

import jax
import jax.numpy as jnp
from jax import lax
from jax.experimental import pallas as pl
from jax.experimental.pallas import tpu as pltpu

RMS_EPS = 1e-6
POOL_WINDOWS = (2, 4, 8, 16)
POOL_HALO = 16
HEAD_DIM = 64
TOP_K = 2
LANES = 128
SUBLANES = 8
KEY_BLOCK = 256
KEY_CHUNK = KEY_BLOCK // SUBLANES
LOG2E = 1.4426950408889634
VMEM_LIMIT = 56 * 1024 * 1024

F32 = jnp.float32
BF16 = jnp.bfloat16


def _cparams(sem):
    return pltpu.CompilerParams(dimension_semantics=sem, vmem_limit_bytes=VMEM_LIMIT)


def _rmsnorm(x, g):
    inv = lax.rsqrt(jnp.mean(x * x, axis=-1, keepdims=True) + RMS_EPS)
    return (x * inv) * g


def _dot(a, b):
    return jnp.dot(a, b, preferred_element_type=F32)


def _silu(g):
    return g * (1.0 / (1.0 + jnp.exp(-g)))


def _pool_kernel(x_ref, g_ref, w_ref, sc_ref, o_ref, l0, l1, l2, l3, *, ts, gw):
    si = pl.program_id(1)
    levels = (l0, l1, l2, l3)
    d = x_ref.shape[-1]

    @pl.when(si == 0)
    def _():
        for lv in levels:
            lv[0:POOL_HALO, :] = jnp.zeros((POOL_HALO, d), F32)

    x = x_ref[0]
    h = _rmsnorm(x, g_ref[...])
    l0[POOL_HALO:POOL_HALO + ts, :] = h
    for k in range(3):
        sh = 1 << k
        c0 = k * gw
        src, dst = levels[k], levels[k + 1]
        dst[POOL_HALO:POOL_HALO + ts, c0:] = (
            src[POOL_HALO:POOL_HALO + ts, c0:] + src[POOL_HALO - sh:POOL_HALO - sh + ts, c0:])
    c3 = 3 * gw
    s16 = l3[POOL_HALO:POOL_HALO + ts, c3:] + l3[POOL_HALO - 8:POOL_HALO - 8 + ts, c3:]

    t = si * ts + lax.broadcasted_iota(jnp.int32, (ts, 1), 0)
    for gi, w in enumerate(POOL_WINDOWS):
        sl = slice(gi * gw, (gi + 1) * gw)
        if gi < 3:
            s = levels[gi + 1][POOL_HALO:POOL_HALO + ts, sl]
        else:
            s = s16
        cnt = jnp.minimum(t + 1, w).astype(F32)
        p = (s / cnt - h[:, sl]).astype(BF16)
        y = _dot(p, w_ref[gi])
        o_ref[0, :, sl] = x[:, sl] + y * sc_ref[:, sl]

    for lv in levels:
        lv[0:POOL_HALO, :] = lv[ts:ts + POOL_HALO, :]


def _pool_layer(x, g, w_bf, scale, *, ts):
    b, s, d = x.shape
    ng, gw, _ = w_bf.shape
    ts = min(ts, s)
    kern = functools.partial(_pool_kernel, ts=ts, gw=gw)
    return pl.pallas_call(
        kern,
        out_shape=jax.ShapeDtypeStruct((b, s, d), F32),
        grid=(b, s // ts),
        in_specs=[
            pl.BlockSpec((1, ts, d), lambda bi, si: (bi, si, 0)),
            pl.BlockSpec((1, d), lambda bi, si: (0, 0)),
            pl.BlockSpec((ng, gw, gw), lambda bi, si: (0, 0, 0)),
            pl.BlockSpec((1, d), lambda bi, si: (0, 0)),
        ],
        out_specs=pl.BlockSpec((1, ts, d), lambda bi, si: (bi, si, 0)),
        scratch_shapes=[pltpu.VMEM((ts + POOL_HALO, d), F32) for _ in range(4)],
        compiler_params=_cparams(("arbitrary", "arbitrary")),
        name="pool_layer",
    )(x, g.reshape(1, d), w_bf, scale.reshape(1, d))


def _ffn_kernel(x_ref, g_ref, wg_ref, wu_ref, wo_ref, o_ref, h_sc):
    j = pl.program_id(1)

    @pl.when(j == 0)
    def _():
        h_sc[...] = _rmsnorm(x_ref[...], g_ref[...]).astype(BF16)

    h = h_sc[...]
    a = _silu(_dot(h, wg_ref[...])) * _dot(h, wu_ref[...])
    part = _dot(a.astype(BF16), wo_ref[...])

    @pl.when(j == 0)
    def _():
        o_ref[...] = x_ref[...] + part

    @pl.when(j > 0)
    def _():
        o_ref[...] += part


def _dense_ffn(x2, g, w_in_bf, w_out_bf, *, tm, tf):
    n, d = x2.shape
    f = w_out_bf.shape[0]
    tm = min(tm, n)
    tf = min(tf, f)
    nj = f // tf
    return pl.pallas_call(
        _ffn_kernel,
        out_shape=jax.ShapeDtypeStruct((n, d), F32),
        grid=(n // tm, nj),
        in_specs=[
            pl.BlockSpec((tm, d), lambda i, j: (i, 0)),
            pl.BlockSpec((1, d), lambda i, j: (0, 0)),
            pl.BlockSpec((d, tf), lambda i, j: (0, j)),
            pl.BlockSpec((d, tf), lambda i, j: (0, j + nj)),
            pl.BlockSpec((tf, d), lambda i, j: (j, 0)),
        ],
        out_specs=pl.BlockSpec((tm, d), lambda i, j: (i, 0)),
        scratch_shapes=[pltpu.VMEM((tm, d), BF16)],
        compiler_params=_cparams(("arbitrary", "arbitrary")),
        name="dense_ffn",
    )(x2, g.reshape(1, d), w_in_bf, w_in_bf, w_out_bf)


def _proj_kernel(x_ref, g_ref, w_ref, o_ref, *, out_scale):
    h = _rmsnorm(x_ref[...], g_ref[...]).astype(BF16)
    y = _dot(h, w_ref[...])
    if out_scale != 1.0:
        y = y * out_scale
    o_ref[...] = y.astype(BF16)


def _norm_proj(x2, g, w_bf, *, tm, out_scale=1.0):
    n, d = x2.shape
    dn = w_bf.shape[1]
    tm = min(tm, n)
    return pl.pallas_call(
        functools.partial(_proj_kernel, out_scale=out_scale),
        out_shape=jax.ShapeDtypeStruct((n, dn), BF16),
        grid=(n // tm,),
        in_specs=[
            pl.BlockSpec((tm, d), lambda i: (i, 0)),
            pl.BlockSpec((1, d), lambda i: (0, 0)),
            pl.BlockSpec((d, dn), lambda i: (0, 0)),
        ],
        out_specs=pl.BlockSpec((tm, dn), lambda i: (i, 0)),
        compiler_params=_cparams(("arbitrary",)),
        name="norm_proj",
    )(x2, g.reshape(1, d), w_bf)


def _oproj_kernel(x_ref, o_in_ref, w_ref, o_ref):
    o_ref[...] = x_ref[...] + _dot(o_in_ref[...], w_ref[...])


def _out_proj(x2, o_bf, w_bf, *, tm):
    n, d = x2.shape
    tm = min(tm, n)
    return pl.pallas_call(
        _oproj_kernel,
        out_shape=jax.ShapeDtypeStruct((n, d), F32),
        grid=(n // tm,),
        in_specs=[
            pl.BlockSpec((tm, d), lambda i: (i, 0)),
            pl.BlockSpec((tm, d), lambda i: (i, 0)),
            pl.BlockSpec((d, d), lambda i: (0, 0)),
        ],
        out_specs=pl.BlockSpec((tm, d), lambda i: (i, 0)),
        compiler_params=_cparams(("arbitrary",)),
        name="out_proj",
    )(x2, o_bf, w_bf)


def _attn_kernel(qt_ref, kp_ref, vt_ref, o_ref, z_sc, d_sc, at_sc, acc_sc, *, tq):
    qi = pl.program_id(2)
    kb = KEY_BLOCK
    nr = kb // SUBLANES
    nd = tq // kb
    qt = qt_ref[0, 0]
    sub = lax.broadcasted_iota(jnp.int32, (SUBLANES, tq), 0)
    j_hi = (qi + 1) * nd - 1

    q_pos = qi * tq + lax.broadcasted_iota(jnp.int32, (SUBLANES, tq), 1)

    def step(j, c, masked):
        slot = j % 2
        zs = z_sc.at[slot]
        acc_sc[...] += _dot(vt_ref[0, 0, jnp.minimum(j + 1, j_hi)], at_sc[...])

        def causal(r):
            return (j * kb + r + KEY_CHUNK * sub) < q_pos

        run = jnp.zeros((SUBLANES, tq), F32)
        for r in reversed(range(nr)):
            sl = slice(r * SUBLANES, (r + 1) * SUBLANES)
            z = zs[sl, :]
            sp = jnp.maximum(z, 0.0) + jnp.log(1.0 + jnp.exp2(jnp.abs(z) * (-LOG2E)))
            if masked:
                sp = jnp.where(causal(r), sp, 0.0)
            run = run + sp
            d_sc[sl, :] = z - run
        z_sc[1 - slot] = _dot(kp_ref[0, 0, jnp.maximum(j - 1, 0)], qt)
        y = jnp.where(sub + 1 < SUBLANES, pltpu.roll(run, SUBLANES - 1, axis=0), 0.0)
        for dd in (1, 2, 4):
            y = y + jnp.where(sub + dd < SUBLANES, pltpu.roll(y, SUBLANES - dd, axis=0), 0.0)
        oc = (y + c) * LOG2E
        for r in range(0, nr, 2):
            pair = []
            for rr in (r, r + 1):
                a = jnp.exp2(d_sc[rr * SUBLANES:(rr + 1) * SUBLANES, :] * LOG2E - oc)
                if masked:
                    a = jnp.where(causal(rr), a, 0.0)
                pair.append(a)
            at_sc[r * SUBLANES:(r + 2) * SUBLANES, :] = jnp.concatenate(pair, axis=0).astype(BF16)
        return c + (y[0:1, :] + run[0:1, :])

    acc_sc[...] = jnp.zeros_like(acc_sc)
    at_sc[...] = jnp.zeros_like(at_sc)
    z_sc[j_hi % 2] = _dot(kp_ref[0, 0, j_hi], qt)
    c = jnp.zeros((1, tq), F32)
    for dj in reversed(range(nd)):
        c = step(qi * nd + dj, c, True)
    c = lax.fori_loop(0, qi * nd, lambda it, cc: step(qi * nd - 1 - it, cc, False), c)
    o_ref[0, 0] = (acc_sc[...] + _dot(vt_ref[0, 0, 0], at_sc[...])).astype(BF16)


def _attention(qt, kp, vt, *, tq):
    b, nh, hd, s = qt.shape
    nkb = kp.shape[2]
    tq = min(tq, s)
    return pl.pallas_call(
        functools.partial(_attn_kernel, tq=tq),
        out_shape=jax.ShapeDtypeStruct((b, nh, hd, s), BF16),
        grid=(b, nh, s // tq),
        in_specs=[
            pl.BlockSpec((1, 1, hd, tq), lambda bi, hi, qi: (bi, hi, 0, qi)),
            pl.BlockSpec((1, 1, nkb, KEY_BLOCK, hd), lambda bi, hi, qi: (bi, hi, 0, 0, 0)),
            pl.BlockSpec((1, 1, nkb, hd, KEY_BLOCK), lambda bi, hi, qi: (bi, hi, 0, 0, 0)),
        ],
        out_specs=pl.BlockSpec((1, 1, hd, tq), lambda bi, hi, qi: (bi, hi, 0, qi)),
        scratch_shapes=[
            pltpu.VMEM((2, KEY_BLOCK, tq), F32),
            pltpu.VMEM((KEY_BLOCK, tq), F32),
            pltpu.VMEM((KEY_BLOCK, tq), BF16),
            pltpu.VMEM((hd, tq), F32),
        ],
        compiler_params=_cparams(("arbitrary", "arbitrary", "arbitrary")),
        name="stickbreak_attn",
    )(qt, kp, vt)


def _split_heads_kv(kv_bf, b, s):
    d = kv_bf.shape[1] // 2
    nh = d // HEAD_DIM
    nkb = s // KEY_BLOCK

    def heads(t):
        t = t.reshape(b, nkb, SUBLANES, KEY_CHUNK, nh, HEAD_DIM)
        return t.transpose(0, 4, 1, 3, 2, 5).reshape(b, nh, nkb, KEY_BLOCK, HEAD_DIM)

    kp = heads(kv_bf[:, :d])
    vt = heads(kv_bf[:, d:]).transpose(0, 1, 2, 4, 3)
    return kp, vt


def _router_kernel(x_ref, g_ref, wr_ref, h_ref, meta_ref, gate_ref, cnt_ref, run_sc, *, n_exp):
    i = pl.program_id(0)
    tm = x_ref.shape[0]

    @pl.when(i == 0)
    def _():
        run_sc[...] = jnp.zeros_like(run_sc)

    h = _rmsnorm(x_ref[...], g_ref[...])
    h_ref[...] = h
    logits = jnp.dot(h, wr_ref[...], preferred_element_type=F32, precision=lax.Precision.HIGHEST)
    lane = lax.broadcasted_iota(jnp.int32, (tm, LANES), 1)
    neg = jnp.float32(-jnp.inf)
    logits = jnp.where(lane < n_exp, logits, neg)
    m1 = jnp.max(logits, axis=-1, keepdims=True)
    i1 = jnp.min(jnp.where(logits == m1, lane, LANES), axis=-1, keepdims=True)
    rest = jnp.where(lane == i1, neg, logits)
    m2 = jnp.max(rest, axis=-1, keepdims=True)
    i2 = jnp.min(jnp.where(rest == m2, lane, LANES), axis=-1, keepdims=True)
    e = jnp.exp(m2 - m1)
    den = 1.0 + e
    g1 = 1.0 / den
    g2 = e / den

    sel1 = lane == i1
    sel2 = lane == i2
    onehot = jnp.where(sel1 | sel2, 1.0, 0.0)
    tri = (lax.broadcasted_iota(jnp.int32, (tm, tm), 1)
           < lax.broadcasted_iota(jnp.int32, (tm, tm), 0)).astype(BF16)
    before = _dot(tri, onehot.astype(BF16)) + run_sc[...]
    r1 = jnp.sum(jnp.where(sel1, before, 0.0), axis=-1, keepdims=True).astype(jnp.int32)
    r2 = jnp.sum(jnp.where(sel2, before, 0.0), axis=-1, keepdims=True).astype(jnp.int32)
    run_sc[...] += jnp.sum(onehot, axis=0, keepdims=True)

    meta = jnp.where(lane == 0, i1, jnp.where(lane == 1, i2, jnp.where(lane == 2, r1, jnp.where(lane == 3, r2, 0))))
    gates = jnp.where(lane == 0, g1, jnp.where(lane == 1, g2, 0.0))
    meta_ref[...] = meta[:, :SUBLANES]
    gate_ref[...] = gates[:, :SUBLANES]
    cnt_ref[...] = run_sc[...].astype(jnp.int32)


def _router(x2, g, w_router, *, tm):
    n, d = x2.shape
    n_exp = w_router.shape[1]
    tm = min(tm, n)
    wr = jnp.zeros((d, LANES), F32).at[:, :n_exp].set(w_router)
    return pl.pallas_call(
        functools.partial(_router_kernel, n_exp=n_exp),
        out_shape=(
            jax.ShapeDtypeStruct((n, d), F32),
            jax.ShapeDtypeStruct((n, SUBLANES), jnp.int32),
            jax.ShapeDtypeStruct((n, SUBLANES), F32),
            jax.ShapeDtypeStruct((1, LANES), jnp.int32),
        ),
        grid=(n // tm,),
        in_specs=[
            pl.BlockSpec((tm, d), lambda i: (i, 0)),
            pl.BlockSpec((1, d), lambda i: (0, 0)),
            pl.BlockSpec((d, LANES), lambda i: (0, 0)),
        ],
        out_specs=(
            pl.BlockSpec((tm, d), lambda i: (i, 0)),
            pl.BlockSpec((tm, SUBLANES), lambda i: (i, 0)),
            pl.BlockSpec((tm, SUBLANES), lambda i: (i, 0)),
            pl.BlockSpec((1, LANES), lambda i: (0, 0)),
        ),
        scratch_shapes=[pltpu.VMEM((1, LANES), F32)],
        compiler_params=_cparams(("arbitrary",)),
        name="moe_router",
    )(x2, g.reshape(1, d), wr)


def _row_copy(src, si, dst, di, sem):
    return pltpu.make_async_copy(src.at[pl.ds(si, 1), :], dst.at[pl.ds(di, 1), :], sem)


def _dispatch_kernel(dest_ref, h_ref, xs_in_ref, xs_ref, sem):
    del xs_in_ref
    tm = h_ref.shape[0]

    def issue(r, carry):
        for k in range(TOP_K):
            _row_copy(h_ref, r, xs_ref, dest_ref[TOP_K * r + k], sem).start()
        return carry

    lax.fori_loop(0, tm, issue, 0)

    def drain(r, carry):
        for k in range(TOP_K):
            _row_copy(h_ref, r, xs_ref, dest_ref[TOP_K * r + k], sem).wait()
        return carry

    lax.fori_loop(0, tm, drain, 0)


def _dispatch(h2, dest, cap, *, tm):
    n, d = h2.shape
    tm = min(tm, n)
    xs0 = jnp.zeros((cap, d), F32)
    return pl.pallas_call(
        _dispatch_kernel,
        out_shape=jax.ShapeDtypeStruct((cap, d), F32),
        grid=(n // tm,),
        in_specs=[
            pl.BlockSpec((TOP_K * tm,), lambda i: (i,), memory_space=pltpu.SMEM),
            pl.BlockSpec((tm, d), lambda i: (i, 0)),
            pl.BlockSpec(memory_space=pl.ANY),
        ],
        out_specs=pl.BlockSpec(memory_space=pl.ANY),
        scratch_shapes=[pltpu.SemaphoreType.DMA],
        input_output_aliases={2: 0},
        compiler_params=_cparams(("arbitrary",)),
        name="moe_dispatch",
    )(dest, h2, xs0)


def _expert_kernel(te_ref, nu_ref, x_ref, wg_ref, wu_ref, wo_ref, y_ref, xb_sc):
    i = pl.program_id(0)
    j = pl.program_id(1)
    used = i < nu_ref[0]

    @pl.when(used & (j == 0))
    def _():
        xb_sc[...] = x_ref[...].astype(BF16)

    @pl.when(used)
    def _():
        xb = xb_sc[...]
        a = _silu(_dot(xb, wg_ref[0])) * _dot(xb, wu_ref[0])
        part = _dot(a.astype(BF16), wo_ref[0])

        @pl.when(j == 0)
        def _():
            y_ref[...] = part

        @pl.when(j > 0)
        def _():
            y_ref[...] += part

    @pl.when(jnp.logical_not(used) & (j == 0))
    def _():
        y_ref[...] = jnp.zeros_like(y_ref)


def _expert_ffn(xs, tile_e, n_used, w_in_bf, w_out_bf, *, tm, tf):
    cap, d = xs.shape
    n_exp, f, _ = w_out_bf.shape
    tf = min(tf, f)
    nj = f // tf
    n_tiles = cap // tm

    def jeff(i, j, nu):
        return jnp.where(i < nu[0], j, nj - 1)

    return pl.pallas_call(
        _expert_kernel,
        out_shape=jax.ShapeDtypeStruct((cap, d), F32),
        grid_spec=pltpu.PrefetchScalarGridSpec(
            num_scalar_prefetch=2,
            grid=(n_tiles, nj),
            in_specs=[
                pl.BlockSpec((tm, d), lambda i, j, te, nu: (jnp.minimum(i, nu[0] - 1), 0)),
                pl.BlockSpec((1, d, tf), lambda i, j, te, nu: (te[i], 0, jeff(i, j, nu))),
                pl.BlockSpec((1, d, tf), lambda i, j, te, nu: (te[i], 0, jeff(i, j, nu) + nj)),
                pl.BlockSpec((1, tf, d), lambda i, j, te, nu: (te[i], jeff(i, j, nu), 0)),
            ],
            out_specs=pl.BlockSpec((tm, d), lambda i, j, te, nu: (i, 0)),
            scratch_shapes=[pltpu.VMEM((tm, d), BF16)],
        ),
        compiler_params=_cparams(("arbitrary", "arbitrary")),
        name="moe_expert_ffn",
    )(tile_e, n_used, xs, w_in_bf, w_in_bf, w_out_bf)


def _combine_kernel(dest_ref, x_ref, gate_ref, gf_ref, ys_ref, o_ref, y_sc, sem, *, final_norm):
    tm = x_ref.shape[0]

    def issue(r, carry):
        for k in range(TOP_K):
            _row_copy(ys_ref, dest_ref[TOP_K * r + k], y_sc.at[k], r, sem).start()
        return carry

    lax.fori_loop(0, tm, issue, 0)

    def drain(r, carry):
        for k in range(TOP_K):
            _row_copy(ys_ref, dest_ref[TOP_K * r + k], y_sc.at[k], r, sem).wait()
        return carry

    lax.fori_loop(0, tm, drain, 0)

    g = gate_ref[...]
    out = x_ref[...] + (y_sc[0] * g[:, 0:1] + y_sc[1] * g[:, 1:2])
    if final_norm:
        out = _rmsnorm(out, gf_ref[...])
    o_ref[...] = out


def _combine(x2, gates, dest, ys, g_final, *, tm, final_norm):
    n, d = x2.shape
    tm = min(tm, n)
    return pl.pallas_call(
        functools.partial(_combine_kernel, final_norm=final_norm),
        out_shape=jax.ShapeDtypeStruct((n, d), F32),
        grid=(n // tm,),
        in_specs=[
            pl.BlockSpec((TOP_K * tm,), lambda i: (i,), memory_space=pltpu.SMEM),
            pl.BlockSpec((tm, d), lambda i: (i, 0)),
            pl.BlockSpec((tm, SUBLANES), lambda i: (i, 0)),
            pl.BlockSpec((1, d), lambda i: (0, 0)),
            pl.BlockSpec(memory_space=pl.ANY),
        ],
        out_specs=pl.BlockSpec((tm, d), lambda i: (i, 0)),
        scratch_shapes=[pltpu.VMEM((TOP_K, tm, d), F32), pltpu.SemaphoreType.DMA],
        compiler_params=_cparams(("arbitrary",)),
        name="moe_combine",
    )(dest, x2, gates, g_final.reshape(1, d), ys)


def _moe_layer(x2, g, w_router, w_in_bf, w_out_bf, g_final, *, final_norm, tm, tm_e, tf):
    n, d = x2.shape
    n_exp = w_router.shape[1]
    tm_e = min(tm_e, n)
    h, meta, gates, cnt = _router(x2, g, w_router, tm=tm)
    counts = cnt[0, :n_exp]
    padded = (counts + tm_e - 1) // tm_e * tm_e
    pend = jnp.cumsum(padded)
    pstart = pend - padded
    dest = (pstart[meta[:, 0:TOP_K]] + meta[:, TOP_K:2 * TOP_K]).reshape(-1).astype(jnp.int32)
    cap = n * TOP_K + n_exp * tm_e
    n_tiles = cap // tm_e
    n_used = (pend[-1] // tm_e).astype(jnp.int32)
    tile_start = jnp.arange(n_tiles, dtype=jnp.int32) * tm_e
    tile_e = jnp.minimum(jnp.searchsorted(pend, tile_start, side="right"), n_exp - 1).astype(jnp.int32)
    tile_e = jnp.where(jnp.arange(n_tiles) < n_used, tile_e, tile_e[n_used - 1])

    xs = _dispatch(h, dest, cap, tm=tm)
    ys = _expert_ffn(xs, tile_e, n_used.reshape(1), w_in_bf, w_out_bf, tm=tm_e, tf=tf)
    return _combine(x2, gates, dest, ys, g_final, tm=tm, final_norm=final_norm)


def _trunk(x, g_mix, g_ffn, pool_w, pool_scale, g_kv, w_kv, w_q, w_o, ffn_in, ffn_out,
           w_router, moe_in, moe_out, g_final, *, ts, tm, tf_dense, tm_e, tf_moe, tq):
    b, s, d = x.shape
    n = b * s
    depth = g_mix.shape[0]
    n_a = pool_w.shape[0]
    nh = d // HEAD_DIM
    kp = vt = None
    for l in range(depth):
        if l < n_a:
            x = _pool_layer(x, g_mix[l], pool_w[l].astype(BF16), pool_scale[l], ts=ts)
        else:
            bi = l - n_a
            x2 = x.reshape(n, d)
            q = _norm_proj(x2, g_mix[l], w_q[bi].astype(BF16), tm=tm, out_scale=HEAD_DIM ** -0.5)
            qt = q.reshape(b, s, nh, HEAD_DIM).transpose(0, 2, 3, 1)
            ot = _attention(qt, kp, vt, tq=tq)
            o = ot.transpose(0, 3, 1, 2).reshape(n, d)
            x = _out_proj(x2, o, w_o[bi].astype(BF16), tm=tm).reshape(b, s, d)
        x2 = x.reshape(n, d)
        if l % 2 == 0:
            x2 = _dense_ffn(x2, g_ffn[l], ffn_in[l // 2].astype(BF16), ffn_out[l // 2].astype(BF16),
                            tm=tm, tf=tf_dense)
        else:
            x2 = _moe_layer(x2, g_ffn[l], w_router[l // 2], moe_in[l // 2].astype(BF16),
                            moe_out[l // 2].astype(BF16), g_final,
                            final_norm=(l == depth - 1), tm=tm, tm_e=tm_e, tf=tf_moe)
        x = x2.reshape(b, s, d)
        if l == n_a - 1:
            kv = _norm_proj(x2, g_kv, w_kv.astype(BF16), tm=tm)
            kp, vt = _split_heads_kv(kv, b, s)
    return x


def kernel(x, g_mix, g_ffn, pool_w, pool_scale, g_kv, w_kv, w_q, w_o, ffn_in, ffn_out,
           w_router, moe_in, moe_out, g_final):
    return _trunk(x, g_mix, g_ffn, pool_w, pool_scale, g_kv, w_kv, w_q, w_o, ffn_in, ffn_out,
                  w_router, moe_in, moe_out, g_final,
                  ts=512, tm=512, tf_dense=1408, tm_e=512, tf_moe=896, tq=256)
```

```python
import functools

import jax
import jax.numpy as jnp
from jax import lax
from jax.experimental import pallas as pl
from jax.experimental.pallas import tpu as pltpu

RMS_EPS = 1e-6
POOL_WINDOWS = (2, 4, 8, 16)
POOL_HALO = 16
HEAD_DIM = 64
TOP_K = 2
LANES = 128
SUBLANES = 8
KEY_BLOCK = 256
KEY_CHUNK = KEY_BLOCK // SUBLANES
LOG2E = 1.4426950408889634
MASKED_SCORE = -1e30
VMEM_LIMIT = 56 * 1024 * 1024

F32 = jnp.float32
BF16 = jnp.bfloat16


def _cparams(sem):
    return pltpu.CompilerParams(dimension_semantics=sem, vmem_limit_bytes=VMEM_LIMIT)


def _rmsnorm(x, g):
    inv = lax.rsqrt(jnp.mean(x * x, axis=-1, keepdims=True) + RMS_EPS)
    return (x * inv) * g


def _dot(a, b):
    return jnp.dot(a, b, preferred_element_type=F32)


def _silu(g):
    return g * (1.0 / (1.0 + jnp.exp(-g)))


def _pool_kernel(x_ref, g_ref, w_ref, sc_ref, o_ref, l0, l1, l2, l3, *, ts, gw):
    si = pl.program_id(1)
    levels = (l0, l1, l2, l3)
    d = x_ref.shape[-1]

    @pl.when(si == 0)
    def _():
        for lv in levels:
            lv[0:POOL_HALO, :] = jnp.zeros((POOL_HALO, d), F32)

    x = x_ref[0]
    h = _rmsnorm(x, g_ref[...])
    l0[POOL_HALO:POOL_HALO + ts, :] = h
    for k in range(3):
        sh = 1 << k
        c0 = k * gw
        src, dst = levels[k], levels[k + 1]
        dst[POOL_HALO:POOL_HALO + ts, c0:] = (
            src[POOL_HALO:POOL_HALO + ts, c0:] + src[POOL_HALO - sh:POOL_HALO - sh + ts, c0:])
    c3 = 3 * gw
    s16 = l3[POOL_HALO:POOL_HALO + ts, c3:] + l3[POOL_HALO - 8:POOL_HALO - 8 + ts, c3:]

    t = si * ts + lax.broadcasted_iota(jnp.int32, (ts, 1), 0)
    for gi, w in enumerate(POOL_WINDOWS):
        sl = slice(gi * gw, (gi + 1) * gw)
        if gi < 3:
            s = levels[gi + 1][POOL_HALO:POOL_HALO + ts, sl]
        else:
            s = s16
        cnt = jnp.minimum(t + 1, w).astype(F32)
        p = (s / cnt - h[:, sl]).astype(BF16)
        y = _dot(p, w_ref[gi])
        o_ref[0, :, sl] = x[:, sl] + y * sc_ref[:, sl]

    for lv in levels:
        lv[0:POOL_HALO, :] = lv[ts:ts + POOL_HALO, :]


def _pool_layer(x, g, w_bf, scale, *, ts):
    b, s, d = x.shape
    ng, gw, _ = w_bf.shape
    ts = min(ts, s)
    kern = functools.partial(_pool_kernel, ts=ts, gw=gw)
    return pl.pallas_call(
        kern,
        out_shape=jax.ShapeDtypeStruct((b, s, d), F32),
        grid=(b, s // ts),
        in_specs=[
            pl.BlockSpec((1, ts, d), lambda bi, si: (bi, si, 0)),
            pl.BlockSpec((1, d), lambda bi, si: (0, 0)),
            pl.BlockSpec((ng, gw, gw), lambda bi, si: (0, 0, 0)),
            pl.BlockSpec((1, d), lambda bi, si: (0, 0)),
        ],
        out_specs=pl.BlockSpec((1, ts, d), lambda bi, si: (bi, si, 0)),
        scratch_shapes=[pltpu.VMEM((ts + POOL_HALO, d), F32) for _ in range(4)],
        compiler_params=_cparams(("arbitrary", "arbitrary")),
        name="pool_layer",
    )(x, g.reshape(1, d), w_bf, scale.reshape(1, d))


def _ffn_kernel(x_ref, g_ref, wg_ref, wu_ref, wo_ref, o_ref, h_sc):
    j = pl.program_id(1)

    @pl.when(j == 0)
    def _():
        h_sc[...] = _rmsnorm(x_ref[...], g_ref[...]).astype(BF16)

    h = h_sc[...]
    a = _silu(_dot(h, wg_ref[...])) * _dot(h, wu_ref[...])
    part = _dot(a.astype(BF16), wo_ref[...])

    @pl.when(j == 0)
    def _():
        o_ref[...] = x_ref[...] + part

    @pl.when(j > 0)
    def _():
        o_ref[...] += part


def _dense_ffn(x2, g, w_in_bf, w_out_bf, *, tm, tf):
    n, d = x2.shape
    f = w_out_bf.shape[0]
    tm = min(tm, n)
    tf = min(tf, f)
    nj = f // tf
    return pl.pallas_call(
        _ffn_kernel,
        out_shape=jax.ShapeDtypeStruct((n, d), F32),
        grid=(n // tm, nj),
        in_specs=[
            pl.BlockSpec((tm, d), lambda i, j: (i, 0)),
            pl.BlockSpec((1, d), lambda i, j: (0, 0)),
            pl.BlockSpec((d, tf), lambda i, j: (0, j)),
            pl.BlockSpec((d, tf), lambda i, j: (0, j + nj)),
            pl.BlockSpec((tf, d), lambda i, j: (j, 0)),
        ],
        out_specs=pl.BlockSpec((tm, d), lambda i, j: (i, 0)),
        scratch_shapes=[pltpu.VMEM((tm, d), BF16)],
        compiler_params=_cparams(("arbitrary", "arbitrary")),
        name="dense_ffn",
    )(x2, g.reshape(1, d), w_in_bf, w_in_bf, w_out_bf)


def _qproj_kernel(x_ref, g_ref, w_ref, o_ref, *, out_scale):
    h = _rmsnorm(x_ref[0], g_ref[...]).astype(BF16)
    y = _dot(h, w_ref[...]) * out_scale
    o_ref[0] = y.T.astype(BF16)


def _q_proj(x, g, w_bf, *, tm, out_scale):
    b, s, d = x.shape
    tm = min(tm, s)
    return pl.pallas_call(
        functools.partial(_qproj_kernel, out_scale=out_scale),
        out_shape=jax.ShapeDtypeStruct((b, d, s), BF16),
        grid=(b, s // tm),
        in_specs=[
            pl.BlockSpec((1, tm, d), lambda bi, si: (bi, si, 0)),
            pl.BlockSpec((1, d), lambda bi, si: (0, 0)),
            pl.BlockSpec((d, d), lambda bi, si: (0, 0)),
        ],
        out_specs=pl.BlockSpec((1, d, tm), lambda bi, si: (bi, 0, si)),
        compiler_params=_cparams(("arbitrary", "arbitrary")),
        name="q_proj",
    )(x, g.reshape(1, d), w_bf)


def _kvproj_kernel(x_ref, g_ref, w_ref, k_ref, vt_ref):
    tm, d = x_ref.shape
    h = _rmsnorm(x_ref[...], g_ref[...]).astype(BF16)
    y = _dot(h, w_ref[...]).astype(BF16)
    row = lax.broadcasted_iota(jnp.int32, (KEY_BLOCK, KEY_BLOCK), 0)
    col = lax.broadcasted_iota(jnp.int32, (KEY_BLOCK, KEY_BLOCK), 1)
    perm = (col == KEY_CHUNK * (row % SUBLANES) + row // SUBLANES).astype(BF16)
    for blk in range(tm // KEY_BLOCK):
        rows = slice(blk * KEY_BLOCK, (blk + 1) * KEY_BLOCK)
        p = _dot(perm, y[rows, :])
        k_ref[rows, :] = p[:, :d].astype(BF16)
        vt_ref[blk] = p[:, d:].T.astype(BF16)


def _kv_proj(x2, g, w_bf, *, tm):
    n, d = x2.shape
    tm = max(min(tm, n), KEY_BLOCK)
    nb = tm // KEY_BLOCK
    return pl.pallas_call(
        _kvproj_kernel,
        out_shape=(jax.ShapeDtypeStruct((n, d), BF16),
                   jax.ShapeDtypeStruct((n // KEY_BLOCK, d, KEY_BLOCK), BF16)),
        grid=(n // tm,),
        in_specs=[
            pl.BlockSpec((tm, d), lambda i: (i, 0)),
            pl.BlockSpec((1, d), lambda i: (0, 0)),
            pl.BlockSpec((d, 2 * d), lambda i: (0, 0)),
        ],
        out_specs=(pl.BlockSpec((tm, d), lambda i: (i, 0)),
                   pl.BlockSpec((nb, d, KEY_BLOCK), lambda i: (i, 0, 0))),
        compiler_params=_cparams(("arbitrary",)),
        name="kv_proj",
    )(x2, g.reshape(1, d), w_bf)


def _oproj_kernel(x_ref, ot_ref, w_ref, o_ref):
    o = ot_ref[0].astype(F32).T.astype(BF16)
    o_ref[0] = x_ref[0] + _dot(o, w_ref[...])


def _out_proj(x, ot_bf, w_bf, *, tm):
    b, s, d = x.shape
    tm = min(tm, s)
    return pl.pallas_call(
        _oproj_kernel,
        out_shape=jax.ShapeDtypeStruct((b, s, d), F32),
        grid=(b, s // tm),
        in_specs=[
            pl.BlockSpec((1, tm, d), lambda bi, si: (bi, si, 0)),
            pl.BlockSpec((1, d, tm), lambda bi, si: (bi, 0, si)),
            pl.BlockSpec((d, d), lambda bi, si: (0, 0)),
        ],
        out_specs=pl.BlockSpec((1, tm, d), lambda bi, si: (bi, si, 0)),
        compiler_params=_cparams(("arbitrary", "arbitrary")),
        name="out_proj",
    )(x, ot_bf, w_bf)


def _attn_kernel(qt_ref, kp_ref, vt_ref, o_ref, z_sc, d_sc, at_sc, acc_sc):
    kb = KEY_BLOCK
    tq = KEY_BLOCK
    nr = kb // SUBLANES
    s = qt_ref.shape[-1]
    odd = pl.program_id(1) % 2
    sub = lax.broadcasted_iota(jnp.int32, (SUBLANES, tq), 0)
    lane = lax.broadcasted_iota(jnp.int32, (SUBLANES, tq), 1)
    mine = (lax.broadcasted_iota(jnp.int32, (2 * HEAD_DIM, tq), 0) // HEAD_DIM) == odd

    def step(j, c, qt, qi, masked):
        slot = j % 2
        zs = z_sc.at[slot]
        acc_sc[...] += _dot(vt_ref[0, jnp.minimum(j + 1, qi)], at_sc[...])
        run = jnp.zeros((SUBLANES, tq), F32)
        for r in reversed(range(nr)):
            sl = slice(r * SUBLANES, (r + 1) * SUBLANES)
            z = zs[sl, :]
            if masked:
                z = jnp.where((j * kb + r + KEY_CHUNK * sub) < (qi * tq + lane), z, MASKED_SCORE)
            sp = jnp.maximum(z, 0.0) + jnp.log(1.0 + jnp.exp2(jnp.abs(z) * (-LOG2E)))
            run = run + sp
            d_sc[sl, :] = z - run
        z_sc[1 - slot] = _dot(kp_ref[0, jnp.maximum(j - 1, 0)], qt)
        y = jnp.where(sub + 1 < SUBLANES, pltpu.roll(run, SUBLANES - 1, axis=0), 0.0)
        for dd in (1, 2, 4):
            y = y + jnp.where(sub + dd < SUBLANES, pltpu.roll(y, SUBLANES - dd, axis=0), 0.0)
        oc = (y + c) * LOG2E
        for r in range(0, nr, 2):
            pair = [jnp.exp2(d_sc[rr * SUBLANES:(rr + 1) * SUBLANES, :] * LOG2E - oc) for rr in (r, r + 1)]
            at_sc[r * SUBLANES:(r + 2) * SUBLANES, :] = jnp.concatenate(pair, axis=0).astype(BF16)
        return c + (y[0:1, :] + run[0:1, :])

    for qi in range(s // tq):
        cols = slice(qi * tq, (qi + 1) * tq)
        qt_pair = qt_ref[0, :, cols]
        qt = jnp.where(mine, qt_pair, jnp.zeros_like(qt_pair))
        acc_sc[...] = jnp.zeros_like(acc_sc)
        at_sc[...] = jnp.zeros_like(at_sc)
        z_sc[qi % 2] = _dot(kp_ref[0, qi], qt)
        c = step(qi, jnp.zeros((1, tq), F32), qt, qi, True)
        if qi > 0:
            c = lax.fori_loop(0, qi, lambda it, cc: step(qi - 1 - it, cc, qt, qi, False), c)
        acc = acc_sc[...] + _dot(vt_ref[0, 0], at_sc[...])
        o_ref[0, :, cols] = jnp.where(odd == 1, acc[HEAD_DIM:, :], acc[:HEAD_DIM, :]).astype(BF16)


def _attention(qt, kp, vt, *, nh):
    b, d, s = qt.shape
    nkb = s // KEY_BLOCK
    pw = 2 * HEAD_DIM
    return pl.pallas_call(
        _attn_kernel,
        out_shape=jax.ShapeDtypeStruct((b, d, s), BF16),
        grid=(b, nh),
        in_specs=[
            pl.BlockSpec((1, pw, s), lambda bi, hi: (bi, hi // 2, 0)),
            pl.BlockSpec((1, nkb, KEY_BLOCK, pw), lambda bi, hi: (bi, 0, 0, hi // 2)),
            pl.BlockSpec((1, nkb, pw, KEY_BLOCK), lambda bi, hi: (bi, 0, hi // 2, 0)),
        ],
        out_specs=pl.BlockSpec((1, HEAD_DIM, s), lambda bi, hi: (bi, hi, 0)),
        scratch_shapes=[
            pltpu.VMEM((2, KEY_BLOCK, KEY_BLOCK), F32),
            pltpu.VMEM((KEY_BLOCK, KEY_BLOCK), F32),
            pltpu.VMEM((KEY_BLOCK, KEY_BLOCK), BF16),
            pltpu.VMEM((pw, KEY_BLOCK), F32),
        ],
        compiler_params=_cparams(("arbitrary", "arbitrary")),
        name="stickbreak_attn",
    )(qt, kp, vt)


def _router_kernel(x_ref, g_ref, wr_ref, h_ref, meta_ref, gate_ref, cnt_ref, run_sc, *, n_exp):
    i = pl.program_id(0)
    tm = x_ref.shape[0]

    @pl.when(i == 0)
    def _():
        run_sc[...] = jnp.zeros_like(run_sc)

    h = _rmsnorm(x_ref[...], g_ref[...])
    h_ref[...] = h
    logits = jnp.dot(h, wr_ref[...], preferred_element_type=F32, precision=lax.Precision.HIGHEST)
    lane = lax.broadcasted_iota(jnp.int32, (tm, LANES), 1)
    neg = jnp.float32(-jnp.inf)
    logits = jnp.where(lane < n_exp, logits, neg)
    m1 = jnp.max(logits, axis=-1, keepdims=True)
    i1 = jnp.min(jnp.where(logits == m1, lane, LANES), axis=-1, keepdims=True)
    rest = jnp.where(lane == i1, neg, logits)
    m2 = jnp.max(rest, axis=-1, keepdims=True)
    i2 = jnp.min(jnp.where(rest == m2, lane, LANES), axis=-1, keepdims=True)
    e = jnp.exp(m2 - m1)
    den = 1.0 + e
    g1 = 1.0 / den
    g2 = e / den

    sel1 = lane == i1
    sel2 = lane == i2
    onehot = jnp.where(sel1 | sel2, 1.0, 0.0)
    tri = (lax.broadcasted_iota(jnp.int32, (tm, tm), 1)
           < lax.broadcasted_iota(jnp.int32, (tm, tm), 0)).astype(BF16)
    before = _dot(tri, onehot.astype(BF16)) + run_sc[...]
    r1 = jnp.sum(jnp.where(sel1, before, 0.0), axis=-1, keepdims=True).astype(jnp.int32)
    r2 = jnp.sum(jnp.where(sel2, before, 0.0), axis=-1, keepdims=True).astype(jnp.int32)
    run_sc[...] += jnp.sum(onehot, axis=0, keepdims=True)

    meta = jnp.where(lane == 0, i1, jnp.where(lane == 1, i2, jnp.where(lane == 2, r1, jnp.where(lane == 3, r2, 0))))
    gates = jnp.where(lane == 0, g1, jnp.where(lane == 1, g2, 0.0))
    meta_ref[...] = meta[:, :SUBLANES]
    gate_ref[...] = gates[:, :SUBLANES]
    cnt_ref[...] = run_sc[...].astype(jnp.int32)


def _router(x2, g, w_router, *, tm):
    n, d = x2.shape
    n_exp = w_router.shape[1]
    tm = min(tm, n)
    wr = jnp.zeros((d, LANES), F32).at[:, :n_exp].set(w_router)
    return pl.pallas_call(
        functools.partial(_router_kernel, n_exp=n_exp),
        out_shape=(
            jax.ShapeDtypeStruct((n, d), F32),
            jax.ShapeDtypeStruct((n, SUBLANES), jnp.int32),
            jax.ShapeDtypeStruct((n, SUBLANES), F32),
            jax.ShapeDtypeStruct((1, LANES), jnp.int32),
        ),
        grid=(n // tm,),
        in_specs=[
            pl.BlockSpec((tm, d), lambda i: (i, 0)),
            pl.BlockSpec((1, d), lambda i: (0, 0)),
            pl.BlockSpec((d, LANES), lambda i: (0, 0)),
        ],
        out_specs=(
            pl.BlockSpec((tm, d), lambda i: (i, 0)),
            pl.BlockSpec((tm, SUBLANES), lambda i: (i, 0)),
            pl.BlockSpec((tm, SUBLANES), lambda i: (i, 0)),
            pl.BlockSpec((1, LANES), lambda i: (0, 0)),
        ),
        scratch_shapes=[pltpu.VMEM((1, LANES), F32)],
        compiler_params=_cparams(("arbitrary",)),
        name="moe_router",
    )(x2, g.reshape(1, d), wr)


def _row_copy(src, si, dst, di, sem):
    return pltpu.make_async_copy(src.at[pl.ds(si, 1), :], dst.at[pl.ds(di, 1), :], sem)


def _dispatch_kernel(dest_ref, h_ref, xs_in_ref, xs_ref, sem):
    del xs_in_ref
    tm = h_ref.shape[0]

    def issue(r, carry):
        for k in range(TOP_K):
            _row_copy(h_ref, r, xs_ref, dest_ref[TOP_K * r + k], sem).start()
        return carry

    lax.fori_loop(0, tm, issue, 0)

    def drain(r, carry):
        for k in range(TOP_K):
            _row_copy(h_ref, r, xs_ref, dest_ref[TOP_K * r + k], sem).wait()
        return carry

    lax.fori_loop(0, tm, drain, 0)


def _dispatch(h2, dest, cap, *, tm):
    n, d = h2.shape
    tm = min(tm, n)
    xs0 = jnp.zeros((cap, d), F32)
    return pl.pallas_call(
        _dispatch_kernel,
        out_shape=jax.ShapeDtypeStruct((cap, d), F32),
        grid=(n // tm,),
        in_specs=[
            pl.BlockSpec((TOP_K * tm,), lambda i: (i,), memory_space=pltpu.SMEM),
            pl.BlockSpec((tm, d), lambda i: (i, 0)),
            pl.BlockSpec(memory_space=pl.ANY),
        ],
        out_specs=pl.BlockSpec(memory_space=pl.ANY),
        scratch_shapes=[pltpu.SemaphoreType.DMA],
        input_output_aliases={2: 0},
        compiler_params=_cparams(("arbitrary",)),
        name="moe_dispatch",
    )(dest, h2, xs0)


def _expert_kernel(te_ref, nu_ref, x_ref, wg_ref, wu_ref, wo_ref, y_ref, xb_sc):
    i = pl.program_id(0)
    j = pl.program_id(1)
    used = i < nu_ref[0]

    @pl.when(used & (j == 0))
    def _():
        xb_sc[...] = x_ref[...].astype(BF16)

    @pl.when(used)
    def _():
        xb = xb_sc[...]
        a = _silu(_dot(xb, wg_ref[0])) * _dot(xb, wu_ref[0])
        part = _dot(a.astype(BF16), wo_ref[0])

        @pl.when(j == 0)
        def _():
            y_ref[...] = part

        @pl.when(j > 0)
        def _():
            y_ref[...] += part

    @pl.when(jnp.logical_not(used) & (j == 0))
    def _():
        y_ref[...] = jnp.zeros_like(y_ref)


def _expert_ffn(xs, tile_e, n_used, w_in_bf, w_out_bf, *, tm, tf):
    cap, d = xs.shape
    n_exp, f, _ = w_out_bf.shape
    tf = min(tf, f)
    nj = f // tf
    n_tiles = cap // tm

    def jeff(i, j, nu):
        return jnp.where(i < nu[0], j, nj - 1)

    return pl.pallas_call(
        _expert_kernel,
        out_shape=jax.ShapeDtypeStruct((cap, d), F32),
        grid_spec=pltpu.PrefetchScalarGridSpec(
            num_scalar_prefetch=2,
            grid=(n_tiles, nj),
            in_specs=[
                pl.BlockSpec((tm, d), lambda i, j, te, nu: (jnp.minimum(i, jnp.maximum(nu[0] - 1, 0)), 0)),
                pl.BlockSpec((1, d, tf), lambda i, j, te, nu: (te[i], 0, jeff(i, j, nu))),
                pl.BlockSpec((1, d, tf), lambda i, j, te, nu: (te[i], 0, jeff(i, j, nu) + nj)),
                pl.BlockSpec((1, tf, d), lambda i, j, te, nu: (te[i], jeff(i, j, nu), 0)),
            ],
            out_specs=pl.BlockSpec((tm, d), lambda i, j, te, nu: (i, 0)),
            scratch_shapes=[pltpu.VMEM((tm, d), BF16)],
        ),
        compiler_params=_cparams(("arbitrary", "arbitrary")),
        name="moe_expert_ffn",
    )(tile_e, n_used, xs, w_in_bf, w_in_bf, w_out_bf)


def _combine_kernel(dest_ref, x_ref, gate_ref, gf_ref, ys_ref, o_ref, y_sc, sem, *, final_norm):
    tm = x_ref.shape[0]

    def issue(r, carry):
        for k in range(TOP_K):
            _row_copy(ys_ref, dest_ref[TOP_K * r + k], y_sc.at[k], r, sem).start()
        return carry

    lax.fori_loop(0, tm, issue, 0)

    def drain(r, carry):
        for k in range(TOP_K):
            _row_copy(ys_ref, dest_ref[TOP_K * r + k], y_sc.at[k], r, sem).wait()
        return carry

    lax.fori_loop(0, tm, drain, 0)

    g = gate_ref[...]
    out = x_ref[...] + (y_sc[0] * g[:, 0:1] + y_sc[1] * g[:, 1:2])
    if final_norm:
        out = _rmsnorm(out, gf_ref[...])
    o_ref[...] = out


def _combine(x2, gates, dest, ys, g_final, *, tm, final_norm):
    n, d = x2.shape
    tm = min(tm, n)
    return pl.pallas_call(
        functools.partial(_combine_kernel, final_norm=final_norm),
        out_shape=jax.ShapeDtypeStruct((n, d), F32),
        grid=(n // tm,),
        in_specs=[
            pl.BlockSpec((TOP_K * tm,), lambda i: (i,), memory_space=pltpu.SMEM),
            pl.BlockSpec((tm, d), lambda i: (i, 0)),
            pl.BlockSpec((tm, SUBLANES), lambda i: (i, 0)),
            pl.BlockSpec((1, d), lambda i: (0, 0)),
            pl.BlockSpec(memory_space=pl.ANY),
        ],
        out_specs=pl.BlockSpec((tm, d), lambda i: (i, 0)),
        scratch_shapes=[pltpu.VMEM((TOP_K, tm, d), F32), pltpu.SemaphoreType.DMA],
        compiler_params=_cparams(("arbitrary",)),
        name="moe_combine",
    )(dest, x2, gates, g_final.reshape(1, d), ys)


def _moe_layer(x2, g, w_router, w_in_bf, w_out_bf, g_final, *, final_norm, tm, tm_e, tf):
    n, d = x2.shape
    n_exp = w_router.shape[1]
    tm_e = min(tm_e, n)
    h, meta, gates, cnt = _router(x2, g, w_router, tm=tm)
    counts = cnt[0, :n_exp]
    padded = (counts + tm_e - 1) // tm_e * tm_e
    pend = jnp.cumsum(padded)
    pstart = pend - padded
    dest = (pstart[meta[:, 0:TOP_K]] + meta[:, TOP_K:2 * TOP_K]).reshape(-1).astype(jnp.int32)
    cap = n * TOP_K + n_exp * tm_e
    n_tiles = cap // tm_e
    n_used = (pend[-1] // tm_e).astype(jnp.int32)
    tile_start = jnp.arange(n_tiles, dtype=jnp.int32) * tm_e
    tile_e = jnp.minimum(jnp.searchsorted(pend, tile_start, side="right"), n_exp - 1).astype(jnp.int32)
    tile_e = jnp.where(jnp.arange(n_tiles) < n_used, tile_e, tile_e[n_used - 1])

    xs = _dispatch(h, dest, cap, tm=tm)
    ys = _expert_ffn(xs, tile_e, n_used.reshape(1), w_in_bf, w_out_bf, tm=tm_e, tf=tf)
    return _combine(x2, gates, dest, ys, g_final, tm=tm, final_norm=final_norm)


def _trunk(x, g_mix, g_ffn, pool_w, pool_scale, g_kv, w_kv, w_q, w_o, ffn_in, ffn_out,
           w_router, moe_in, moe_out, g_final, *, ts, tm, tf_dense, tm_e, tf_moe):
    b, s, d = x.shape
    n = b * s
    depth = g_mix.shape[0]
    n_a = pool_w.shape[0]
    nh = d // HEAD_DIM
    kp = vt = None
    for l in range(depth):
        if l < n_a:
            x = _pool_layer(x, g_mix[l], pool_w[l].astype(BF16), pool_scale[l], ts=ts)
        else:
            bi = l - n_a
            qt = _q_proj(x, g_mix[l], w_q[bi].astype(BF16), tm=tm, out_scale=HEAD_DIM ** -0.5)
            ot = _attention(qt, kp, vt, nh=nh)
            x = _out_proj(x, ot, w_o[bi].astype(BF16), tm=tm)
        x2 = x.reshape(n, d)
        if l % 2 == 0:
            x2 = _dense_ffn(x2, g_ffn[l], ffn_in[l // 2].astype(BF16), ffn_out[l // 2].astype(BF16),
                            tm=tm, tf=tf_dense)
        else:
            x2 = _moe_layer(x2, g_ffn[l], w_router[l // 2], moe_in[l // 2].astype(BF16),
                            moe_out[l // 2].astype(BF16), g_final,
                            final_norm=(l == depth - 1), tm=tm, tm_e=tm_e, tf=tf_moe)
        x = x2.reshape(b, s, d)
        if l == n_a - 1:
            kp, vt = _kv_proj(x2, g_kv, w_kv.astype(BF16), tm=tm)
            kp = kp.reshape(b, s // KEY_BLOCK, KEY_BLOCK, d)
            vt = vt.reshape(b, s // KEY_BLOCK, d, KEY_BLOCK)
    return x


def kernel(x, g_mix, g_ffn, pool_w, pool_scale, g_kv, w_kv, w_q, w_o, ffn_in, ffn_out,
           w_router, moe_in, moe_out, g_final):
    return _trunk(x, g_mix, g_ffn, pool_w, pool_scale, g_kv, w_kv, w_q, w_o, ffn_in, ffn_out,
                  w_router, moe_in, moe_out, g_final,
                  ts=512, tm=512, tf_dense=1408, tm_e=512, tf_moe=896)
```

```python
import functools

import jax
import jax.numpy as jnp
from jax import lax
from jax.experimental import pallas as pl
from jax.experimental.pallas import tpu as pltpu

RMS_EPS = 1e-6
POOL_WINDOWS = (2, 4, 8, 16)
POOL_HALO = 16
HEAD_DIM = 64
TOP_K = 2
LANES = 128
SUBLANES = 8
KEY_BLOCK = 256
KEY_CHUNK = KEY_BLOCK // SUBLANES
ATTN_PAIRS = 2
FFN_CHUNK = 256
VMEM_LIMIT = 56 * 1024 * 1024

F32 = jnp.float32
BF16 = jnp.bfloat16


def _cparams(sem):
    return pltpu.CompilerParams(dimension_semantics=sem, vmem_limit_bytes=VMEM_LIMIT)


def _rmsnorm(x, g):
    inv = lax.rsqrt(jnp.mean(x * x, axis=-1, keepdims=True) + RMS_EPS)
    return (x * inv) * g


def _dot(a, b):
    return jnp.dot(a, b, preferred_element_type=F32)


def _silu(g):
    return g * (1.0 / (1.0 + jnp.exp(-g)))


def _pool_kernel(x_ref, g_ref, w_ref, sc_ref, o_ref, l0, l1, l2, l3, *, ts, gw):
    si = pl.program_id(1)
    levels = (l0, l1, l2, l3)
    d = x_ref.shape[-1]

    @pl.when(si == 0)
    def _():
        for lv in levels:
            lv[0:POOL_HALO, :] = jnp.zeros((POOL_HALO, d), F32)

    x = x_ref[0]
    h = _rmsnorm(x, g_ref[...])
    l0[POOL_HALO:POOL_HALO + ts, :] = h
    for k in range(3):
        sh = 1 << k
        c0 = k * gw
        src, dst = levels[k], levels[k + 1]
        dst[POOL_HALO:POOL_HALO + ts, c0:] = (
            src[POOL_HALO:POOL_HALO + ts, c0:] + src[POOL_HALO - sh:POOL_HALO - sh + ts, c0:])
    c3 = 3 * gw
    s16 = l3[POOL_HALO:POOL_HALO + ts, c3:] + l3[POOL_HALO - 8:POOL_HALO - 8 + ts, c3:]

    t = si * ts + lax.broadcasted_iota(jnp.int32, (ts, 1), 0)
    for gi, w in enumerate(POOL_WINDOWS):
        sl = slice(gi * gw, (gi + 1) * gw)
        if gi < 3:
            s = levels[gi + 1][POOL_HALO:POOL_HALO + ts, sl]
        else:
            s = s16
        cnt = jnp.minimum(t + 1, w).astype(F32)
        p = (s / cnt - h[:, sl]).astype(BF16)
        y = _dot(p, w_ref[gi])
        o_ref[0, :, sl] = x[:, sl] + y * sc_ref[:, sl]

    for lv in levels:
        lv[0:POOL_HALO, :] = lv[ts:ts + POOL_HALO, :]


def _pool_layer(x, g, w_bf, scale, *, ts):
    b, s, d = x.shape
    ng, gw, _ = w_bf.shape
    ts = min(ts, s)
    kern = functools.partial(_pool_kernel, ts=ts, gw=gw)
    return pl.pallas_call(
        kern,
        out_shape=jax.ShapeDtypeStruct((b, s, d), F32),
        grid=(b, s // ts),
        in_specs=[
            pl.BlockSpec((1, ts, d), lambda bi, si: (bi, si, 0)),
            pl.BlockSpec((1, d), lambda bi, si: (0, 0)),
            pl.BlockSpec((ng, gw, gw), lambda bi, si: (0, 0, 0)),
            pl.BlockSpec((1, d), lambda bi, si: (0, 0)),
        ],
        out_specs=pl.BlockSpec((1, ts, d), lambda bi, si: (bi, si, 0)),
        scratch_shapes=[pltpu.VMEM((ts + POOL_HALO, d), F32) for _ in range(4)],
        compiler_params=_cparams(("arbitrary", "arbitrary")),
        name="pool_layer",
    )(x, g.reshape(1, d), w_bf, scale.reshape(1, d))


def _swiglu_accumulate(h, wg_ref, wu_ref, wo_ref, o_ref):
    tf = wo_ref.shape[0]
    for c0 in range(0, tf, FFN_CHUNK):
        cs = slice(c0, min(c0 + FFN_CHUNK, tf))
        a = _silu(_dot(h, wg_ref[:, cs])) * _dot(h, wu_ref[:, cs])
        o_ref[...] += _dot(a.astype(BF16), wo_ref[cs, :])


def _ffn_kernel(x_ref, g_ref, wg_ref, wu_ref, wo_ref, o_ref, h_sc):
    @pl.when(pl.program_id(1) == 0)
    def _():
        h_sc[...] = _rmsnorm(x_ref[...], g_ref[...]).astype(BF16)
        o_ref[...] = x_ref[...]

    _swiglu_accumulate(h_sc[...], wg_ref, wu_ref, wo_ref, o_ref)


def _dense_ffn(x2, g, w_in_bf, w_out_bf, *, tm, tf):
    n, d = x2.shape
    f = w_out_bf.shape[0]
    tm = min(tm, n)
    tf = min(tf, f)
    nj = f // tf
    return pl.pallas_call(
        _ffn_kernel,
        out_shape=jax.ShapeDtypeStruct((n, d), F32),
        grid=(n // tm, nj),
        in_specs=[
            pl.BlockSpec((tm, d), lambda i, j: (i, 0)),
            pl.BlockSpec((1, d), lambda i, j: (0, 0)),
            pl.BlockSpec((d, tf), lambda i, j: (0, j)),
            pl.BlockSpec((d, tf), lambda i, j: (0, j + nj)),
            pl.BlockSpec((tf, d), lambda i, j: (j, 0)),
        ],
        out_specs=pl.BlockSpec((tm, d), lambda i, j: (i, 0)),
        scratch_shapes=[pltpu.VMEM((tm, d), BF16)],
        compiler_params=_cparams(("arbitrary", "arbitrary")),
        name="dense_ffn",
    )(x2, g.reshape(1, d), w_in_bf, w_in_bf, w_out_bf)


def _qproj_kernel(x_ref, g_ref, w_ref, o_ref, *, out_scale):
    h = _rmsnorm(x_ref[0], g_ref[...]).astype(BF16)
    y = _dot(h, w_ref[...]) * out_scale
    o_ref[0] = y.T.astype(BF16)


def _q_proj(x, g, w_bf, *, tm, out_scale):
    b, s, d = x.shape
    tm = min(tm, s)
    return pl.pallas_call(
        functools.partial(_qproj_kernel, out_scale=out_scale),
        out_shape=jax.ShapeDtypeStruct((b, d, s), BF16),
        grid=(b, s // tm),
        in_specs=[
            pl.BlockSpec((1, tm, d), lambda bi, si: (bi, si, 0)),
            pl.BlockSpec((1, d), lambda bi, si: (0, 0)),
            pl.BlockSpec((d, d), lambda bi, si: (0, 0)),
        ],
        out_specs=pl.BlockSpec((1, d, tm), lambda bi, si: (bi, 0, si)),
        compiler_params=_cparams(("arbitrary", "arbitrary")),
        name="q_proj",
    )(x, g.reshape(1, d), w_bf)


def _kvproj_kernel(x_ref, g_ref, w_ref, k_ref, vt_ref):
    tm, d = x_ref.shape
    h = _rmsnorm(x_ref[...], g_ref[...]).astype(BF16)
    y = _dot(h, w_ref[...]).astype(BF16)
    row = lax.broadcasted_iota(jnp.int32, (KEY_BLOCK, KEY_BLOCK), 0)
    col = lax.broadcasted_iota(jnp.int32, (KEY_BLOCK, KEY_BLOCK), 1)
    perm = (col == KEY_CHUNK * (row % SUBLANES) + row // SUBLANES).astype(BF16)
    for blk in range(tm // KEY_BLOCK):
        rows = slice(blk * KEY_BLOCK, (blk + 1) * KEY_BLOCK)
        p = _dot(perm, y[rows, :])
        k_ref[rows, :] = p[:, :d].astype(BF16)
        vt_ref[blk] = p[:, d:].T.astype(BF16)


def _kv_proj(x2, g, w_bf, *, tm):
    n, d = x2.shape
    tm = max(min(tm, n), KEY_BLOCK)
    nb = tm // KEY_BLOCK
    return pl.pallas_call(
        _kvproj_kernel,
        out_shape=(jax.ShapeDtypeStruct((n, d), BF16),
                   jax.ShapeDtypeStruct((n // KEY_BLOCK, d, KEY_BLOCK), BF16)),
        grid=(n // tm,),
        in_specs=[
            pl.BlockSpec((tm, d), lambda i: (i, 0)),
            pl.BlockSpec((1, d), lambda i: (0, 0)),
            pl.BlockSpec((d, 2 * d), lambda i: (0, 0)),
        ],
        out_specs=(pl.BlockSpec((tm, d), lambda i: (i, 0)),
                   pl.BlockSpec((nb, d, KEY_BLOCK), lambda i: (i, 0, 0))),
        compiler_params=_cparams(("arbitrary",)),
        name="kv_proj",
    )(x2, g.reshape(1, d), w_bf)


def _oproj_kernel(x_ref, ot_ref, w_ref, o_ref):
    o = ot_ref[0].astype(F32).T.astype(BF16)
    o_ref[0] = x_ref[0] + _dot(o, w_ref[...])


def _out_proj(x, ot_bf, w_bf, *, tm):
    b, s, d = x.shape
    tm = min(tm, s)
    return pl.pallas_call(
        _oproj_kernel,
        out_shape=jax.ShapeDtypeStruct((b, s, d), F32),
        grid=(b, s // tm),
        in_specs=[
            pl.BlockSpec((1, tm, d), lambda bi, si: (bi, si, 0)),
            pl.BlockSpec((1, d, tm), lambda bi, si: (bi, 0, si)),
            pl.BlockSpec((d, d), lambda bi, si: (0, 0)),
        ],
        out_specs=pl.BlockSpec((1, tm, d), lambda bi, si: (bi, si, 0)),
        compiler_params=_cparams(("arbitrary", "arbitrary")),
        name="out_proj",
    )(x, ot_bf, w_bf)


def _attn_kernel(qt_ref, kp_ref, vt_ref, o_ref, z_sc, d_sc, at_sc, acc_sc):
    kb = KEY_BLOCK
    tq = KEY_BLOCK
    w = 2 * tq
    nr = kb // SUBLANES
    pw = 2 * HEAD_DIM
    n_pairs = qt_ref.shape[1] // pw
    s = qt_ref.shape[-1]
    sub = lax.broadcasted_iota(jnp.int32, (SUBLANES, w), 0)
    q_off = lax.broadcasted_iota(jnp.int32, (SUBLANES, w), 1) % tq
    head_a = lax.broadcasted_iota(jnp.int32, (pw, tq), 0) < HEAD_DIM

    def step(j, cs, qts, qi, masked):
        slot = j % 2
        jv = jnp.minimum(j + 1, qi)
        for p in range(n_pairs):
            acc_sc[p] += _dot(vt_ref[0, jv, p * pw:(p + 1) * pw, :], at_sc[p])
        runs = []
        for p in range(n_pairs):
            run = jnp.ones((SUBLANES, w), F32)
            for r in reversed(range(nr)):
                sl = slice(r * SUBLANES, (r + 1) * SUBLANES)
                t = jnp.tanh(z_sc[slot, p, sl, :]) * 0.5
                if masked:
                    t = jnp.where((r + KEY_CHUNK * sub) < q_off, t, -0.5)
                d_sc[p, sl, :] = (0.5 + t) * run
                run = run * (0.5 - t)
            runs.append(run)
        jk = jnp.maximum(j - 1, 0)
        for p in range(n_pairs):
            z_sc[1 - slot, p] = _dot(kp_ref[0, jk, :, p * pw:(p + 1) * pw], qts[p])
        out = []
        for p in range(n_pairs):
            run = runs[p]
            y = jnp.where(sub + 1 < SUBLANES, pltpu.roll(run, SUBLANES - 1, axis=0), 1.0)
            for dd in (1, 2, 4):
                y = y * jnp.where(sub + dd < SUBLANES, pltpu.roll(y, SUBLANES - dd, axis=0), 1.0)
            oc = y * cs[p]
            for r in range(0, nr, 2):
                two = [d_sc[p, rr * SUBLANES:(rr + 1) * SUBLANES, :] * oc for rr in (r, r + 1)]
                at_sc[p, r * SUBLANES:(r + 2) * SUBLANES, :] = jnp.concatenate(two, axis=0).astype(BF16)
            out.append(cs[p] * (y[0:1, :] * run[0:1, :]))
        return tuple(out)

    for qi in range(s // tq):
        cols = slice(qi * tq, (qi + 1) * tq)
        qts = []
        for p in range(n_pairs):
            q2 = qt_ref[0, p * pw:(p + 1) * pw, cols]
            zero = jnp.zeros_like(q2)
            qts.append(jnp.concatenate([jnp.where(head_a, q2, zero), jnp.where(head_a, zero, q2)], axis=1))
            z_sc[qi % 2, p] = _dot(kp_ref[0, qi, :, p * pw:(p + 1) * pw], qts[p])
        acc_sc[...] = jnp.zeros_like(acc_sc)
        at_sc[...] = jnp.zeros_like(at_sc)
        ones = tuple(jnp.ones((1, w), F32) for _ in range(n_pairs))
        cs = step(qi, ones, qts, qi, True)
        if qi > 0:
            cs = lax.fori_loop(0, qi, lambda it, cc: step(qi - 1 - it, cc, qts, qi, False), cs)
        for p in range(n_pairs):
            acc = acc_sc[p] + _dot(vt_ref[0, 0, p * pw:(p + 1) * pw, :], at_sc[p])
            o_ref[0, p * pw:p * pw + HEAD_DIM, cols] = acc[:HEAD_DIM, :tq].astype(BF16)
            o_ref[0, p * pw + HEAD_DIM:(p + 1) * pw, cols] = acc[HEAD_DIM:, tq:].astype(BF16)


def _attention(qt, kp, vt):
    b, d, s = qt.shape
    nkb = s // KEY_BLOCK
    gw = ATTN_PAIRS * 2 * HEAD_DIM
    w = 2 * KEY_BLOCK
    return pl.pallas_call(
        _attn_kernel,
        out_shape=jax.ShapeDtypeStruct((b, d, s), BF16),
        grid=(b, d // gw),
        in_specs=[
            pl.BlockSpec((1, gw, s), lambda bi, gi: (bi, gi, 0)),
            pl.BlockSpec((1, nkb, KEY_BLOCK, gw), lambda bi, gi: (bi, 0, 0, gi)),
            pl.BlockSpec((1, nkb, gw, KEY_BLOCK), lambda bi, gi: (bi, 0, gi, 0)),
        ],
        out_specs=pl.BlockSpec((1, gw, s), lambda bi, gi: (bi, gi, 0)),
        scratch_shapes=[
            pltpu.VMEM((2, ATTN_PAIRS, KEY_BLOCK, w), F32),
            pltpu.VMEM((ATTN_PAIRS, KEY_BLOCK, w), F32),
            pltpu.VMEM((ATTN_PAIRS, KEY_BLOCK, w), BF16),
            pltpu.VMEM((ATTN_PAIRS, 2 * HEAD_DIM, w), F32),
        ],
        compiler_params=_cparams(("arbitrary", "arbitrary")),
        name="stickbreak_attn",
    )(qt, kp, vt)


def _router_kernel(x_ref, g_ref, wr_ref, h_ref, meta_ref, gate_ref, cnt_ref, run_sc, *, n_exp):
    i = pl.program_id(0)
    tm = x_ref.shape[0]

    @pl.when(i == 0)
    def _():
        run_sc[...] = jnp.zeros_like(run_sc)

    h = _rmsnorm(x_ref[...], g_ref[...])
    h_ref[...] = h
    logits = jnp.dot(h, wr_ref[...], preferred_element_type=F32, precision=lax.Precision.HIGHEST)
    lane = lax.broadcasted_iota(jnp.int32, (tm, LANES), 1)
    neg = jnp.float32(-jnp.inf)
    logits = jnp.where(lane < n_exp, logits, neg)
    m1 = jnp.max(logits, axis=-1, keepdims=True)
    i1 = jnp.min(jnp.where(logits == m1, lane, LANES), axis=-1, keepdims=True)
    rest = jnp.where(lane == i1, neg, logits)
    m2 = jnp.max(rest, axis=-1, keepdims=True)
    i2 = jnp.min(jnp.where(rest == m2, lane, LANES), axis=-1, keepdims=True)
    e = jnp.exp(m2 - m1)
    den = 1.0 + e
    g1 = 1.0 / den
    g2 = e / den

    sel1 = lane == i1
    sel2 = lane == i2
    onehot = jnp.where(sel1 | sel2, 1.0, 0.0)
    tri = (lax.broadcasted_iota(jnp.int32, (tm, tm), 1)
           < lax.broadcasted_iota(jnp.int32, (tm, tm), 0)).astype(BF16)
    before = _dot(tri, onehot.astype(BF16)) + run_sc[...]
    r1 = jnp.sum(jnp.where(sel1, before, 0.0), axis=-1, keepdims=True).astype(jnp.int32)
    r2 = jnp.sum(jnp.where(sel2, before, 0.0), axis=-1, keepdims=True).astype(jnp.int32)
    run_sc[...] += jnp.sum(onehot, axis=0, keepdims=True)

    meta = jnp.where(lane == 0, i1, jnp.where(lane == 1, i2, jnp.where(lane == 2, r1, jnp.where(lane == 3, r2, 0))))
    gates = jnp.where(lane == 0, g1, jnp.where(lane == 1, g2, 0.0))
    meta_ref[...] = meta[:, :SUBLANES]
    gate_ref[...] = gates[:, :SUBLANES]
    cnt_ref[...] = run_sc[...].astype(jnp.int32)


def _router(x2, g, w_router, *, tm):
    n, d = x2.shape
    n_exp = w_router.shape[1]
    tm = min(tm, n)
    wr = jnp.zeros((d, LANES), F32).at[:, :n_exp].set(w_router)
    return pl.pallas_call(
        functools.partial(_router_kernel, n_exp=n_exp),
        out_shape=(
            jax.ShapeDtypeStruct((n, d), F32),
            jax.ShapeDtypeStruct((n, SUBLANES), jnp.int32),
            jax.ShapeDtypeStruct((n, SUBLANES), F32),
            jax.ShapeDtypeStruct((1, LANES), jnp.int32),
        ),
        grid=(n // tm,),
        in_specs=[
            pl.BlockSpec((tm, d), lambda i: (i, 0)),
            pl.BlockSpec((1, d), lambda i: (0, 0)),
            pl.BlockSpec((d, LANES), lambda i: (0, 0)),
        ],
        out_specs=(
            pl.BlockSpec((tm, d), lambda i: (i, 0)),
            pl.BlockSpec((tm, SUBLANES), lambda i: (i, 0)),
            pl.BlockSpec((tm, SUBLANES), lambda i: (i, 0)),
            pl.BlockSpec((1, LANES), lambda i: (0, 0)),
        ),
        scratch_shapes=[pltpu.VMEM((1, LANES), F32)],
        compiler_params=_cparams(("arbitrary",)),
        name="moe_router",
    )(x2, g.reshape(1, d), wr)


def _row_copy(src, si, dst, di, sem):
    return pltpu.make_async_copy(src.at[pl.ds(si, 1), :], dst.at[pl.ds(di, 1), :], sem)


def _dispatch_kernel(dest_ref, h_ref, xs_in_ref, xs_ref, sem):
    del xs_in_ref
    tm = h_ref.shape[0]

    def issue(r, carry):
        for k in range(TOP_K):
            _row_copy(h_ref, r, xs_ref, dest_ref[TOP_K * r + k], sem).start()
        return carry

    lax.fori_loop(0, tm, issue, 0)

    def drain(r, carry):
        for k in range(TOP_K):
            _row_copy(h_ref, r, xs_ref, dest_ref[TOP_K * r + k], sem).wait()
        return carry

    lax.fori_loop(0, tm, drain, 0)


def _dispatch(h2, dest, cap, *, tm):
    n, d = h2.shape
    tm = min(tm, n)
    xs0 = jnp.zeros((cap, d), F32)
    return pl.pallas_call(
        _dispatch_kernel,
        out_shape=jax.ShapeDtypeStruct((cap, d), F32),
        grid=(n // tm,),
        in_specs=[
            pl.BlockSpec((TOP_K * tm,), lambda i: (i,), memory_space=pltpu.SMEM),
            pl.BlockSpec((tm, d), lambda i: (i, 0)),
            pl.BlockSpec(memory_space=pl.ANY),
        ],
        out_specs=pl.BlockSpec(memory_space=pl.ANY),
        scratch_shapes=[pltpu.SemaphoreType.DMA],
        input_output_aliases={2: 0},
        compiler_params=_cparams(("arbitrary",)),
        name="moe_dispatch",
    )(dest, h2, xs0)


def _expert_kernel(te_ref, nu_ref, x_ref, wg_ref, wu_ref, wo_ref, y_ref, xb_sc):
    i = pl.program_id(0)
    j = pl.program_id(1)
    used = i < nu_ref[0]

    @pl.when(j == 0)
    def _():
        xb_sc[...] = x_ref[...].astype(BF16)
        y_ref[...] = jnp.zeros_like(y_ref)

    @pl.when(used)
    def _():
        _swiglu_accumulate(xb_sc[...], wg_ref.at[0], wu_ref.at[0], wo_ref.at[0], y_ref)


def _expert_ffn(xs, tile_e, n_used, w_in_bf, w_out_bf, *, tm, tf):
    cap, d = xs.shape
    n_exp, f, _ = w_out_bf.shape
    tf = min(tf, f)
    nj = f // tf
    n_tiles = cap // tm

    def jeff(i, j, nu):
        return jnp.where(i < nu[0], j, nj - 1)

    return pl.pallas_call(
        _expert_kernel,
        out_shape=jax.ShapeDtypeStruct((cap, d), F32),
        grid_spec=pltpu.PrefetchScalarGridSpec(
            num_scalar_prefetch=2,
            grid=(n_tiles, nj),
            in_specs=[
                pl.BlockSpec((tm, d), lambda i, j, te, nu: (jnp.minimum(i, jnp.maximum(nu[0] - 1, 0)), 0)),
                pl.BlockSpec((1, d, tf), lambda i, j, te, nu: (te[i], 0, jeff(i, j, nu))),
                pl.BlockSpec((1, d, tf), lambda i, j, te, nu: (te[i], 0, jeff(i, j, nu) + nj)),
                pl.BlockSpec((1, tf, d), lambda i, j, te, nu: (te[i], jeff(i, j, nu), 0)),
            ],
            out_specs=pl.BlockSpec((tm, d), lambda i, j, te, nu: (i, 0)),
            scratch_shapes=[pltpu.VMEM((tm, d), BF16)],
        ),
        compiler_params=_cparams(("arbitrary", "arbitrary")),
        name="moe_expert_ffn",
    )(tile_e, n_used, xs, w_in_bf, w_in_bf, w_out_bf)


def _combine_kernel(dest_ref, x_ref, gate_ref, gf_ref, ys_ref, o_ref, y_sc, sem, *, final_norm):
    tm = x_ref.shape[0]

    def issue(r, carry):
        for k in range(TOP_K):
            _row_copy(ys_ref, dest_ref[TOP_K * r + k], y_sc.at[k], r, sem).start()
        return carry

    lax.fori_loop(0, tm, issue, 0)

    def drain(r, carry):
        for k in range(TOP_K):
            _row_copy(ys_ref, dest_ref[TOP_K * r + k], y_sc.at[k], r, sem).wait()
        return carry

    lax.fori_loop(0, tm, drain, 0)

    g = gate_ref[...]
    out = x_ref[...] + (y_sc[0] * g[:, 0:1] + y_sc[1] * g[:, 1:2])
    if final_norm:
        out = _rmsnorm(out, gf_ref[...])
    o_ref[...] = out


def _combine(x2, gates, dest, ys, g_final, *, tm, final_norm):
    n, d = x2.shape
    tm = min(tm, n)
    return pl.pallas_call(
        functools.partial(_combine_kernel, final_norm=final_norm),
        out_shape=jax.ShapeDtypeStruct((n, d), F32),
        grid=(n // tm,),
        in_specs=[
            pl.BlockSpec((TOP_K * tm,), lambda i: (i,), memory_space=pltpu.SMEM),
            pl.BlockSpec((tm, d), lambda i: (i, 0)),
            pl.BlockSpec((tm, SUBLANES), lambda i: (i, 0)),
            pl.BlockSpec((1, d), lambda i: (0, 0)),
            pl.BlockSpec(memory_space=pl.ANY),
        ],
        out_specs=pl.BlockSpec((tm, d), lambda i: (i, 0)),
        scratch_shapes=[pltpu.VMEM((TOP_K, tm, d), F32), pltpu.SemaphoreType.DMA],
        compiler_params=_cparams(("arbitrary",)),
        name="moe_combine",
    )(dest, x2, gates, g_final.reshape(1, d), ys)


def _moe_layer(x2, g, w_router, w_in_bf, w_out_bf, g_final, *, final_norm, tm, tm_e, tf):
    n, d = x2.shape
    n_exp = w_router.shape[1]
    tm_e = min(tm_e, n)
    h, meta, gates, cnt = _router(x2, g, w_router, tm=tm)
    counts = cnt[0, :n_exp]
    padded = (counts + tm_e - 1) // tm_e * tm_e
    pend = jnp.cumsum(padded)
    pstart = pend - padded
    dest = (pstart[meta[:, 0:TOP_K]] + meta[:, TOP_K:2 * TOP_K]).reshape(-1).astype(jnp.int32)
    cap = n * TOP_K + n_exp * tm_e
    n_tiles = cap // tm_e
    n_used = (pend[-1] // tm_e).astype(jnp.int32)
    tile_start = jnp.arange(n_tiles, dtype=jnp.int32) * tm_e
    tile_e = jnp.minimum(jnp.searchsorted(pend, tile_start, side="right"), n_exp - 1).astype(jnp.int32)
    tile_e = jnp.where(jnp.arange(n_tiles) < n_used, tile_e, tile_e[n_used - 1])

    xs = _dispatch(h, dest, cap, tm=tm)
    ys = _expert_ffn(xs, tile_e, n_used.reshape(1), w_in_bf, w_out_bf, tm=tm_e, tf=tf)
    return _combine(x2, gates, dest, ys, g_final, tm=tm, final_norm=final_norm)


def _trunk(x, g_mix, g_ffn, pool_w, pool_scale, g_kv, w_kv, w_q, w_o, ffn_in, ffn_out,
           w_router, moe_in, moe_out, g_final, *, ts, tm, tf_dense, tm_e, tf_moe):
    b, s, d = x.shape
    n = b * s
    depth = g_mix.shape[0]
    n_a = pool_w.shape[0]
    nh = d // HEAD_DIM
    kp = vt = None
    for l in range(depth):
        if l < n_a:
            x = _pool_layer(x, g_mix[l], pool_w[l].astype(BF16), pool_scale[l], ts=ts)
        else:
            bi = l - n_a
            qt = _q_proj(x, g_mix[l], w_q[bi].astype(BF16), tm=tm, out_scale=0.5 * HEAD_DIM ** -0.5)
            ot = _attention(qt, kp, vt)
            x = _out_proj(x, ot, w_o[bi].astype(BF16), tm=tm)
        x2 = x.reshape(n, d)
        if l % 2 == 0:
            x2 = _dense_ffn(x2, g_ffn[l], ffn_in[l // 2].astype(BF16), ffn_out[l // 2].astype(BF16),
                            tm=tm, tf=tf_dense)
        else:
            x2 = _moe_layer(x2, g_ffn[l], w_router[l // 2], moe_in[l // 2].astype(BF16),
                            moe_out[l // 2].astype(BF16), g_final,
                            final_norm=(l == depth - 1), tm=tm, tm_e=tm_e, tf=tf_moe)
        x = x2.reshape(b, s, d)
        if l == n_a - 1:
            kp, vt = _kv_proj(x2, g_kv, w_kv.astype(BF16), tm=tm)
            kp = kp.reshape(b, s // KEY_BLOCK, KEY_BLOCK, d)
            vt = vt.reshape(b, s // KEY_BLOCK, d, KEY_BLOCK)
    return x


def kernel(x, g_mix, g_ffn, pool_w, pool_scale, g_kv, w_kv, w_q, w_o, ffn_in, ffn_out,
           w_router, moe_in, moe_out, g_final):
    return _trunk(x, g_mix, g_ffn, pool_w, pool_scale, g_kv, w_kv, w_q, w_o, ffn_in, ffn_out,
                  w_router, moe_in, moe_out, g_final,
                  ts=512, tm=512, tf_dense=2816, tm_e=512, tf_moe=1792)
```

```python
import functools

import jax
import jax.numpy as jnp
from jax import lax
from jax.experimental import pallas as pl
from jax.experimental.pallas import tpu as pltpu

RMS_EPS = 1e-6
POOL_WINDOWS = (2, 4, 8, 16)
POOL_HALO = 16
HEAD_DIM = 64
TOP_K = 2
LANES = 128
SUBLANES = 8
KEY_BLOCK = 256
KEY_CHUNK = KEY_BLOCK // SUBLANES
ATTN_PAIRS = 2
FFN_CHUNK = 256
VMEM_LIMIT = 56 * 1024 * 1024

F32 = jnp.float32
BF16 = jnp.bfloat16


def _cparams(sem):
    return pltpu.CompilerParams(dimension_semantics=sem, vmem_limit_bytes=VMEM_LIMIT)


def _rmsnorm(x, g):
    inv = lax.rsqrt(jnp.mean(x * x, axis=-1, keepdims=True) + RMS_EPS)
    return (x * inv) * g


def _dot(a, b):
    return jnp.dot(a, b, preferred_element_type=F32)


def _silu(g):
    return g * (1.0 / (1.0 + jnp.exp(-g)))


def _pool_kernel(x_ref, g_ref, w_ref, sc_ref, o_ref, l0, l1, l2, l3, *, ts, gw):
    si = pl.program_id(1)
    levels = (l0, l1, l2, l3)
    d = x_ref.shape[-1]

    @pl.when(si == 0)
    def _():
        for lv in levels:
            lv[0:POOL_HALO, :] = jnp.zeros((POOL_HALO, d), F32)

    x = x_ref[0]
    h = _rmsnorm(x, g_ref[...])
    l0[POOL_HALO:POOL_HALO + ts, :] = h
    for k in range(3):
        sh = 1 << k
        c0 = k * gw
        src, dst = levels[k], levels[k + 1]
        dst[POOL_HALO:POOL_HALO + ts, c0:] = (
            src[POOL_HALO:POOL_HALO + ts, c0:] + src[POOL_HALO - sh:POOL_HALO - sh + ts, c0:])
    c3 = 3 * gw
    s16 = l3[POOL_HALO:POOL_HALO + ts, c3:] + l3[POOL_HALO - 8:POOL_HALO - 8 + ts, c3:]

    t = si * ts + lax.broadcasted_iota(jnp.int32, (ts, 1), 0)
    for gi, w in enumerate(POOL_WINDOWS):
        sl = slice(gi * gw, (gi + 1) * gw)
        if gi < 3:
            s = levels[gi + 1][POOL_HALO:POOL_HALO + ts, sl]
        else:
            s = s16
        cnt = jnp.minimum(t + 1, w).astype(F32)
        p = (s / cnt - h[:, sl]).astype(BF16)
        y = _dot(p, w_ref[gi])
        o_ref[0, :, sl] = x[:, sl] + y * sc_ref[:, sl]

    for lv in levels:
        lv[0:POOL_HALO, :] = lv[ts:ts + POOL_HALO, :]


def _pool_layer(x, g, w_bf, scale, *, ts):
    b, s, d = x.shape
    ng, gw, _ = w_bf.shape
    ts = min(ts, s)
    kern = functools.partial(_pool_kernel, ts=ts, gw=gw)
    return pl.pallas_call(
        kern,
        out_shape=jax.ShapeDtypeStruct((b, s, d), F32),
        grid=(b, s // ts),
        in_specs=[
            pl.BlockSpec((1, ts, d), lambda bi, si: (bi, si, 0)),
            pl.BlockSpec((1, d), lambda bi, si: (0, 0)),
            pl.BlockSpec((ng, gw, gw), lambda bi, si: (0, 0, 0)),
            pl.BlockSpec((1, d), lambda bi, si: (0, 0)),
        ],
        out_specs=pl.BlockSpec((1, ts, d), lambda bi, si: (bi, si, 0)),
        scratch_shapes=[pltpu.VMEM((ts + POOL_HALO, d), F32) for _ in range(4)],
        compiler_params=_cparams(("arbitrary", "arbitrary")),
        name="pool_layer",
    )(x, g.reshape(1, d), w_bf, scale.reshape(1, d))


def _swiglu_accumulate(h, wg_ref, wu_ref, wo_ref, o_ref, between=None):
    tf = wo_ref.shape[0]
    for ci, c0 in enumerate(range(0, tf, FFN_CHUNK)):
        cs = slice(c0, min(c0 + FFN_CHUNK, tf))
        a = _silu(_dot(h, wg_ref[:, cs])) * _dot(h, wu_ref[:, cs])
        o_ref[...] += _dot(a.astype(BF16), wo_ref[cs, :])
        if between is not None:
            between(ci)


def _ffn_kernel(x_ref, g_ref, wg_ref, wu_ref, wo_ref, o_ref, h_sc):
    @pl.when(pl.program_id(1) == 0)
    def _():
        h_sc[...] = _rmsnorm(x_ref[...], g_ref[...]).astype(BF16)
        o_ref[...] = x_ref[...]

    _swiglu_accumulate(h_sc[...], wg_ref, wu_ref, wo_ref, o_ref)


def _dense_ffn(x2, g, w_in_bf, w_out_bf, *, tm, tf):
    n, d = x2.shape
    f = w_out_bf.shape[0]
    tm = min(tm, n)
    tf = min(tf, f)
    nj = f // tf
    return pl.pallas_call(
        _ffn_kernel,
        out_shape=jax.ShapeDtypeStruct((n, d), F32),
        grid=(n // tm, nj),
        in_specs=[
            pl.BlockSpec((tm, d), lambda i, j: (i, 0)),
            pl.BlockSpec((1, d), lambda i, j: (0, 0)),
            pl.BlockSpec((d, tf), lambda i, j: (0, j)),
            pl.BlockSpec((d, tf), lambda i, j: (0, j + nj)),
            pl.BlockSpec((tf, d), lambda i, j: (j, 0)),
        ],
        out_specs=pl.BlockSpec((tm, d), lambda i, j: (i, 0)),
        scratch_shapes=[pltpu.VMEM((tm, d), BF16)],
        compiler_params=_cparams(("arbitrary", "arbitrary")),
        name="dense_ffn",
    )(x2, g.reshape(1, d), w_in_bf, w_in_bf, w_out_bf)


def _qproj_kernel(x_ref, g_ref, w_ref, o_ref, *, out_scale):
    h = _rmsnorm(x_ref[0], g_ref[...]).astype(BF16)
    y = _dot(h, w_ref[...]) * out_scale
    o_ref[0] = y.T.astype(BF16)


def _q_proj(x, g, w_bf, *, tm, out_scale):
    b, s, d = x.shape
    tm = min(tm, s)
    return pl.pallas_call(
        functools.partial(_qproj_kernel, out_scale=out_scale),
        out_shape=jax.ShapeDtypeStruct((b, d, s), BF16),
        grid=(b, s // tm),
        in_specs=[
            pl.BlockSpec((1, tm, d), lambda bi, si: (bi, si, 0)),
            pl.BlockSpec((1, d), lambda bi, si: (0, 0)),
            pl.BlockSpec((d, d), lambda bi, si: (0, 0)),
        ],
        out_specs=pl.BlockSpec((1, d, tm), lambda bi, si: (bi, 0, si)),
        compiler_params=_cparams(("arbitrary", "arbitrary")),
        name="q_proj",
    )(x, g.reshape(1, d), w_bf)


def _kvproj_kernel(x_ref, g_ref, w_ref, k_ref, vt_ref):
    tm, d = x_ref.shape
    h = _rmsnorm(x_ref[...], g_ref[...]).astype(BF16)
    y = _dot(h, w_ref[...]).astype(BF16)
    row = lax.broadcasted_iota(jnp.int32, (KEY_BLOCK, KEY_BLOCK), 0)
    col = lax.broadcasted_iota(jnp.int32, (KEY_BLOCK, KEY_BLOCK), 1)
    perm = (col == KEY_CHUNK * (row % SUBLANES) + row // SUBLANES).astype(BF16)
    for blk in range(tm // KEY_BLOCK):
        rows = slice(blk * KEY_BLOCK, (blk + 1) * KEY_BLOCK)
        p = _dot(perm, y[rows, :])
        k_ref[rows, :] = p[:, :d].astype(BF16)
        vt_ref[blk] = p[:, d:].T.astype(BF16)


def _kv_proj(x2, g, w_bf, *, tm):
    n, d = x2.shape
    tm = max(min(tm, n), KEY_BLOCK)
    nb = tm // KEY_BLOCK
    return pl.pallas_call(
        _kvproj_kernel,
        out_shape=(jax.ShapeDtypeStruct((n, d), BF16),
                   jax.ShapeDtypeStruct((n // KEY_BLOCK, d, KEY_BLOCK), BF16)),
        grid=(n // tm,),
        in_specs=[
            pl.BlockSpec((tm, d), lambda i: (i, 0)),
            pl.BlockSpec((1, d), lambda i: (0, 0)),
            pl.BlockSpec((d, 2 * d), lambda i: (0, 0)),
        ],
        out_specs=(pl.BlockSpec((tm, d), lambda i: (i, 0)),
                   pl.BlockSpec((nb, d, KEY_BLOCK), lambda i: (i, 0, 0))),
        compiler_params=_cparams(("arbitrary",)),
        name="kv_proj",
    )(x2, g.reshape(1, d), w_bf)


def _oproj_kernel(x_ref, ot_ref, w_ref, o_ref):
    o = ot_ref[0].astype(F32).T.astype(BF16)
    o_ref[0] = x_ref[0] + _dot(o, w_ref[...])


def _out_proj(x, ot_bf, w_bf, *, tm):
    b, s, d = x.shape
    tm = min(tm, s)
    return pl.pallas_call(
        _oproj_kernel,
        out_shape=jax.ShapeDtypeStruct((b, s, d), F32),
        grid=(b, s // tm),
        in_specs=[
            pl.BlockSpec((1, tm, d), lambda bi, si: (bi, si, 0)),
            pl.BlockSpec((1, d, tm), lambda bi, si: (bi, 0, si)),
            pl.BlockSpec((d, d), lambda bi, si: (0, 0)),
        ],
        out_specs=pl.BlockSpec((1, tm, d), lambda bi, si: (bi, si, 0)),
        compiler_params=_cparams(("arbitrary", "arbitrary")),
        name="out_proj",
    )(x, ot_bf, w_bf)


def _attn_kernel(qt_ref, kp_ref, vt_ref, o_ref, z_sc, d_sc, at_sc, acc_sc):
    kb = KEY_BLOCK
    tq = KEY_BLOCK
    w = 2 * tq
    nr = kb // SUBLANES
    pw = 2 * HEAD_DIM
    n_pairs = qt_ref.shape[1] // pw
    s = qt_ref.shape[-1]
    sub = lax.broadcasted_iota(jnp.int32, (SUBLANES, w), 0)
    q_off = lax.broadcasted_iota(jnp.int32, (SUBLANES, w), 1) % tq
    head_a = lax.broadcasted_iota(jnp.int32, (pw, tq), 0) < HEAD_DIM

    def step(j, cs, qts, qi, masked):
        slot = j % 2
        jv = jnp.minimum(j + 1, qi)
        for p in range(n_pairs):
            acc_sc[p] += _dot(vt_ref[0, jv, p * pw:(p + 1) * pw, :], at_sc[p])
        runs = []
        for p in range(n_pairs):
            run = jnp.ones((SUBLANES, w), F32)
            for r in reversed(range(nr)):
                sl = slice(r * SUBLANES, (r + 1) * SUBLANES)
                t = jnp.tanh(z_sc[slot, p, sl, :]) * 0.5
                if masked:
                    t = jnp.where((r + KEY_CHUNK * sub) < q_off, t, -0.5)
                d_sc[p, sl, :] = (0.5 + t) * run
                run = run * (0.5 - t)
            runs.append(run)
        jk = jnp.maximum(j - 1, 0)
        for p in range(n_pairs):
            z_sc[1 - slot, p] = _dot(kp_ref[0, jk, :, p * pw:(p + 1) * pw], qts[p])
        out = []
        for p in range(n_pairs):
            run = runs[p]
            y = jnp.where(sub + 1 < SUBLANES, pltpu.roll(run, SUBLANES - 1, axis=0), 1.0)
            for dd in (1, 2, 4):
                y = y * jnp.where(sub + dd < SUBLANES, pltpu.roll(y, SUBLANES - dd, axis=0), 1.0)
            oc = y * cs[p]
            for r in range(0, nr, 2):
                two = [d_sc[p, rr * SUBLANES:(rr + 1) * SUBLANES, :] * oc for rr in (r, r + 1)]
                at_sc[p, r * SUBLANES:(r + 2) * SUBLANES, :] = jnp.concatenate(two, axis=0).astype(BF16)
            out.append(cs[p] * (y[0:1, :] * run[0:1, :]))
        return tuple(out)

    for qi in range(s // tq):
        cols = slice(qi * tq, (qi + 1) * tq)
        qts = []
        for p in range(n_pairs):
            q2 = qt_ref[0, p * pw:(p + 1) * pw, cols]
            zero = jnp.zeros_like(q2)
            qts.append(jnp.concatenate([jnp.where(head_a, q2, zero), jnp.where(head_a, zero, q2)], axis=1))
            z_sc[qi % 2, p] = _dot(kp_ref[0, qi, :, p * pw:(p + 1) * pw], qts[p])
        acc_sc[...] = jnp.zeros_like(acc_sc)
        at_sc[...] = jnp.zeros_like(at_sc)
        ones = tuple(jnp.ones((1, w), F32) for _ in range(n_pairs))
        cs = step(qi, ones, qts, qi, True)
        if qi > 0:
            cs = lax.fori_loop(0, qi, lambda it, cc: step(qi - 1 - it, cc, qts, qi, False), cs)
        for p in range(n_pairs):
            acc = acc_sc[p] + _dot(vt_ref[0, 0, p * pw:(p + 1) * pw, :], at_sc[p])
            o_ref[0, p * pw:p * pw + HEAD_DIM, cols] = acc[:HEAD_DIM, :tq].astype(BF16)
            o_ref[0, p * pw + HEAD_DIM:(p + 1) * pw, cols] = acc[HEAD_DIM:, tq:].astype(BF16)


def _attention(qt, kp, vt):
    b, d, s = qt.shape
    nkb = s // KEY_BLOCK
    gw = ATTN_PAIRS * 2 * HEAD_DIM
    w = 2 * KEY_BLOCK
    return pl.pallas_call(
        _attn_kernel,
        out_shape=jax.ShapeDtypeStruct((b, d, s), BF16),
        grid=(b, d // gw),
        in_specs=[
            pl.BlockSpec((1, gw, s), lambda bi, gi: (bi, gi, 0)),
            pl.BlockSpec((1, nkb, KEY_BLOCK, gw), lambda bi, gi: (bi, 0, 0, gi)),
            pl.BlockSpec((1, nkb, gw, KEY_BLOCK), lambda bi, gi: (bi, 0, gi, 0)),
        ],
        out_specs=pl.BlockSpec((1, gw, s), lambda bi, gi: (bi, gi, 0)),
        scratch_shapes=[
            pltpu.VMEM((2, ATTN_PAIRS, KEY_BLOCK, w), F32),
            pltpu.VMEM((ATTN_PAIRS, KEY_BLOCK, w), F32),
            pltpu.VMEM((ATTN_PAIRS, KEY_BLOCK, w), BF16),
            pltpu.VMEM((ATTN_PAIRS, 2 * HEAD_DIM, w), F32),
        ],
        compiler_params=_cparams(("arbitrary", "arbitrary")),
        name="stickbreak_attn",
    )(qt, kp, vt)


def _router_kernel(x_ref, g_ref, wr_ref, h_ref, meta_ref, gate_ref, cnt_ref, run_sc, *, n_exp):
    i = pl.program_id(0)
    tm = x_ref.shape[0]

    @pl.when(i == 0)
    def _():
        run_sc[...] = jnp.zeros_like(run_sc)

    h = _rmsnorm(x_ref[...], g_ref[...])
    h_ref[...] = h
    logits = jnp.dot(h, wr_ref[...], preferred_element_type=F32, precision=lax.Precision.HIGHEST)
    lane = lax.broadcasted_iota(jnp.int32, (tm, LANES), 1)
    neg = jnp.float32(-jnp.inf)
    logits = jnp.where(lane < n_exp, logits, neg)
    m1 = jnp.max(logits, axis=-1, keepdims=True)
    i1 = jnp.min(jnp.where(logits == m1, lane, LANES), axis=-1, keepdims=True)
    rest = jnp.where(lane == i1, neg, logits)
    m2 = jnp.max(rest, axis=-1, keepdims=True)
    i2 = jnp.min(jnp.where(rest == m2, lane, LANES), axis=-1, keepdims=True)
    e = jnp.exp(m2 - m1)
    den = 1.0 + e
    g1 = 1.0 / den
    g2 = e / den

    sel1 = lane == i1
    sel2 = lane == i2
    onehot = jnp.where(sel1 | sel2, 1.0, 0.0)
    tri = (lax.broadcasted_iota(jnp.int32, (tm, tm), 1)
           < lax.broadcasted_iota(jnp.int32, (tm, tm), 0)).astype(BF16)
    before = _dot(tri, onehot.astype(BF16)) + run_sc[...]
    r1 = jnp.sum(jnp.where(sel1, before, 0.0), axis=-1, keepdims=True).astype(jnp.int32)
    r2 = jnp.sum(jnp.where(sel2, before, 0.0), axis=-1, keepdims=True).astype(jnp.int32)
    run_sc[...] += jnp.sum(onehot, axis=0, keepdims=True)

    meta = jnp.where(lane == 0, i1, jnp.where(lane == 1, i2, jnp.where(lane == 2, r1, jnp.where(lane == 3, r2, 0))))
    gates = jnp.where(lane == 0, g1, jnp.where(lane == 1, g2, 0.0))
    meta_ref[...] = meta[:, :SUBLANES]
    gate_ref[...] = gates[:, :SUBLANES]
    cnt_ref[...] = run_sc[...].astype(jnp.int32)


def _router(x2, g, w_router, *, tm):
    n, d = x2.shape
    n_exp = w_router.shape[1]
    tm = min(tm, n)
    wr = jnp.zeros((d, LANES), F32).at[:, :n_exp].set(w_router)
    return pl.pallas_call(
        functools.partial(_router_kernel, n_exp=n_exp),
        out_shape=(
            jax.ShapeDtypeStruct((n, d), F32),
            jax.ShapeDtypeStruct((n, SUBLANES), jnp.int32),
            jax.ShapeDtypeStruct((n, SUBLANES), F32),
            jax.ShapeDtypeStruct((1, LANES), jnp.int32),
        ),
        grid=(n // tm,),
        in_specs=[
            pl.BlockSpec((tm, d), lambda i: (i, 0)),
            pl.BlockSpec((1, d), lambda i: (0, 0)),
            pl.BlockSpec((d, LANES), lambda i: (0, 0)),
        ],
        out_specs=(
            pl.BlockSpec((tm, d), lambda i: (i, 0)),
            pl.BlockSpec((tm, SUBLANES), lambda i: (i, 0)),
            pl.BlockSpec((tm, SUBLANES), lambda i: (i, 0)),
            pl.BlockSpec((1, LANES), lambda i: (0, 0)),
        ),
        scratch_shapes=[pltpu.VMEM((1, LANES), F32)],
        compiler_params=_cparams(("arbitrary",)),
        name="moe_router",
    )(x2, g.reshape(1, d), wr)


def _expert_kernel(te_ref, nu_ref, first_ref, nxt_ref, prv_ref, h_ref, wg_ref, wu_ref, wo_ref, y2_ref,
                   xbuf, xb_sc, ybuf, sem_g, sem_s):
    i = pl.program_id(0)
    j = pl.program_id(1)
    n_i = pl.num_programs(0)
    n_j = pl.num_programs(1)
    tm, d = xb_sc.shape
    per_step = tm // n_j
    cur = i % 2
    oth = 1 - cur
    used = i < nu_ref[0]

    def gather(tok_ref, r, slot):
        return pltpu.make_async_copy(
            h_ref.at[pl.ds(tok_ref[r], 1), :], xbuf.at[slot, pl.ds(r, 1), :], sem_g.at[slot])

    def scatter(dst_ref, r, slot):
        return pltpu.make_async_copy(
            ybuf.at[slot, pl.ds(r, 1), :], y2_ref.at[pl.ds(dst_ref[r], 1), :], sem_s.at[slot])

    def wait_gathers(slot):
        pltpu.make_async_copy(h_ref.at[pl.ds(0, tm), :], xbuf.at[slot], sem_g.at[slot]).wait()

    def wait_scatters(slot):
        pltpu.make_async_copy(ybuf.at[slot], y2_ref.at[pl.ds(0, tm), :], sem_s.at[slot]).wait()

    def issue(rr):
        r = j * per_step + rr
        gather(nxt_ref, r, oth).start()
        scatter(prv_ref, r, oth).start()

    @pl.when((i == 0) & (j == 0))
    def _():
        ybuf[...] = jnp.zeros_like(ybuf)

        def first(r, carry):
            gather(first_ref, r, 0).start()
            return carry

        lax.fori_loop(0, tm, first, 0)

    @pl.when(j == 0)
    def _():
        wait_gathers(cur)

        @pl.when(i > 0)
        def _():
            wait_scatters(cur)

        xb_sc[...] = xbuf[cur].astype(BF16)
        ybuf[cur] = jnp.zeros((tm, d), F32)

    @pl.when(used)
    def _():
        n_chunks = pl.cdiv(wo_ref.shape[1], FFN_CHUNK)
        per_chunk = pl.cdiv(per_step, n_chunks)

        def between(ci):
            for rr in range(ci * per_chunk, min((ci + 1) * per_chunk, per_step)):
                issue(rr)

        _swiglu_accumulate(xb_sc[...], wg_ref.at[0], wu_ref.at[0], wo_ref.at[0], ybuf.at[cur], between)

    @pl.when(jnp.logical_not(used))
    def _():
        def body(rr, carry):
            issue(rr)
            return carry

        lax.fori_loop(0, per_step, body, 0)

    @pl.when((i == n_i - 1) & (j == n_j - 1))
    def _():
        wait_gathers(oth)
        wait_scatters(oth)


def _expert_ffn(h2, src_tok, dst_row, tile_e, n_used, w_in_bf, w_out_bf, *, tm, tf):
    n, d = h2.shape
    n_exp, f, _ = w_out_bf.shape
    tf = min(tf, f)
    nj = f // tf
    n_tiles = src_tok.shape[0] // tm - 2

    def jeff(i, j, nu):
        return jnp.where(i < nu[0], j, nj - 1)

    smem = functools.partial(pl.BlockSpec, (tm,), memory_space=pltpu.SMEM)
    return pl.pallas_call(
        _expert_kernel,
        out_shape=jax.ShapeDtypeStruct((TOP_K * n + 2 * tm, d), F32),
        grid_spec=pltpu.PrefetchScalarGridSpec(
            num_scalar_prefetch=2,
            grid=(n_tiles, nj),
            in_specs=[
                smem(lambda i, j, te, nu: (1,)),
                smem(lambda i, j, te, nu: (i + 2,)),
                smem(lambda i, j, te, nu: (i,)),
                pl.BlockSpec(memory_space=pl.ANY),
                pl.BlockSpec((1, d, tf), lambda i, j, te, nu: (te[i], 0, jeff(i, j, nu))),
                pl.BlockSpec((1, d, tf), lambda i, j, te, nu: (te[i], 0, jeff(i, j, nu) + nj)),
                pl.BlockSpec((1, tf, d), lambda i, j, te, nu: (te[i], jeff(i, j, nu), 0)),
            ],
            out_specs=pl.BlockSpec(memory_space=pl.ANY),
            scratch_shapes=[
                pltpu.VMEM((2, tm, d), F32),
                pltpu.VMEM((tm, d), BF16),
                pltpu.VMEM((2, tm, d), F32),
                pltpu.SemaphoreType.DMA((2,)),
                pltpu.SemaphoreType.DMA((2,)),
            ],
        ),
        compiler_params=_cparams(("arbitrary", "arbitrary")),
        name="moe_expert_ffn",
    )(tile_e, n_used, src_tok, src_tok, dst_row, h2, w_in_bf, w_in_bf, w_out_bf)


def _combine_kernel(x_ref, gate_ref, gf_ref, y2_ref, o_ref, *, final_norm):
    d = x_ref.shape[1]
    g = gate_ref[...]
    out = x_ref[...] + (y2_ref[:, :d] * g[:, 0:1] + y2_ref[:, d:] * g[:, 1:2])
    if final_norm:
        out = _rmsnorm(out, gf_ref[...])
    o_ref[...] = out


def _combine(x2, gates, y2, g_final, *, tm, final_norm):
    n, d = x2.shape
    tm = min(tm, n)
    return pl.pallas_call(
        functools.partial(_combine_kernel, final_norm=final_norm),
        out_shape=jax.ShapeDtypeStruct((n, d), F32),
        grid=(n // tm,),
        in_specs=[
            pl.BlockSpec((tm, d), lambda i: (i, 0)),
            pl.BlockSpec((tm, SUBLANES), lambda i: (i, 0)),
            pl.BlockSpec((1, d), lambda i: (0, 0)),
            pl.BlockSpec((tm, TOP_K * d), lambda i: (i, 0)),
        ],
        out_specs=pl.BlockSpec((tm, d), lambda i: (i, 0)),
        compiler_params=_cparams(("arbitrary",)),
        name="moe_combine",
    )(x2, gates, g_final.reshape(1, d), y2.reshape(-1, TOP_K * d))


def _moe_layer(x2, g, w_router, w_in_bf, w_out_bf, g_final, *, final_norm, tm, tm_e, tf):
    n, d = x2.shape
    n_exp = w_router.shape[1]
    tm_e = min(tm_e, n)
    h, meta, gates, cnt = _router(x2, g, w_router, tm=tm)
    counts = cnt[0, :n_exp]
    padded = (counts + tm_e - 1) // tm_e * tm_e
    pend = jnp.cumsum(padded)
    pstart = pend - padded
    dest = (pstart[meta[:, 0:TOP_K]] + meta[:, TOP_K:2 * TOP_K]).reshape(-1).astype(jnp.int32)
    cap = n * TOP_K + n_exp * tm_e
    n_tiles = cap // tm_e
    n_used = (pend[-1] // tm_e).astype(jnp.int32)
    tile_start = jnp.arange(n_tiles, dtype=jnp.int32) * tm_e
    tile_e = jnp.minimum(jnp.searchsorted(pend, tile_start, side="right"), n_exp - 1).astype(jnp.int32)
    tile_e = jnp.where(jnp.arange(n_tiles) < n_used, tile_e, tile_e[n_used - 1])
    rows = jnp.arange(cap + 2 * tm_e, dtype=jnp.int32)
    slots = jnp.full((cap + 2 * tm_e,), -1, jnp.int32).at[dest + tm_e].set(
        jnp.arange(n * TOP_K, dtype=jnp.int32))
    src_tok = jnp.maximum(slots, 0) // TOP_K
    trash = n * TOP_K + (rows // tm_e % 2) * tm_e + rows % tm_e
    dst_row = jnp.where(slots < 0, trash, slots)

    y2 = _expert_ffn(h, src_tok, dst_row, tile_e, n_used.reshape(1), w_in_bf, w_out_bf, tm=tm_e, tf=tf)
    return _combine(x2, gates, y2, g_final, tm=tm, final_norm=final_norm)


def _trunk(x, g_mix, g_ffn, pool_w, pool_scale, g_kv, w_kv, w_q, w_o, ffn_in, ffn_out,
           w_router, moe_in, moe_out, g_final, *, ts, tm, tf_dense, tm_e, tf_moe):
    b, s, d = x.shape
    n = b * s
    depth = g_mix.shape[0]
    n_a = pool_w.shape[0]
    nh = d // HEAD_DIM
    kp = vt = None
    for l in range(depth):
        if l < n_a:
            x = _pool_layer(x, g_mix[l], pool_w[l].astype(BF16), pool_scale[l], ts=ts)
        else:
            bi = l - n_a
            qt = _q_proj(x, g_mix[l], w_q[bi].astype(BF16), tm=tm, out_scale=0.5 * HEAD_DIM ** -0.5)
            ot = _attention(qt, kp, vt)
            x = _out_proj(x, ot, w_o[bi].astype(BF16), tm=tm)
        x2 = x.reshape(n, d)
        if l % 2 == 0:
            x2 = _dense_ffn(x2, g_ffn[l], ffn_in[l // 2].astype(BF16), ffn_out[l // 2].astype(BF16),
                            tm=tm, tf=tf_dense)
        else:
            x2 = _moe_layer(x2, g_ffn[l], w_router[l // 2], moe_in[l // 2].astype(BF16),
                            moe_out[l // 2].astype(BF16), g_final,
                            final_norm=(l == depth - 1), tm=tm, tm_e=tm_e, tf=tf_moe)
        x = x2.reshape(b, s, d)
        if l == n_a - 1:
            kp, vt = _kv_proj(x2, g_kv, w_kv.astype(BF16), tm=tm)
            kp = kp.reshape(b, s // KEY_BLOCK, KEY_BLOCK, d)
            vt = vt.reshape(b, s // KEY_BLOCK, d, KEY_BLOCK)
    return x


def kernel(x, g_mix, g_ffn, pool_w, pool_scale, g_kv, w_kv, w_q, w_o, ffn_in, ffn_out,
           w_router, moe_in, moe_out, g_final):
    return _trunk(x, g_mix, g_ffn, pool_w, pool_scale, g_kv, w_kv, w_q, w_o, ffn_in, ffn_out,
                  w_router, moe_in, moe_out, g_final,
                  ts=512, tm=512, tf_dense=2816, tm_e=1024, tf_moe=1792)
```

```python
import functools

import jax
import jax.numpy as jnp
from jax import lax
from jax.experimental import pallas as pl
from jax.experimental.pallas import tpu as pltpu

RMS_EPS = 1e-6
POOL_WINDOWS = (2, 4, 8, 16)
POOL_HALO = 16
HEAD_DIM = 64
TOP_K = 2
LANES = 128
SUBLANES = 8
KEY_BLOCK = 256
KEY_CHUNK = KEY_BLOCK // SUBLANES
ATTN_PAIRS = 2
FFN_CHUNK = 256
VMEM_LIMIT = 56 * 1024 * 1024

F32 = jnp.float32
BF16 = jnp.bfloat16


def _cparams(sem):
    return pltpu.CompilerParams(dimension_semantics=sem, vmem_limit_bytes=VMEM_LIMIT)


def _rmsnorm(x, g):
    inv = lax.rsqrt(jnp.mean(x * x, axis=-1, keepdims=True) + RMS_EPS)
    return (x * inv) * g


def _dot(a, b):
    return jnp.dot(a, b, preferred_element_type=F32)


def _silu(g):
    return g * (1.0 / (1.0 + jnp.exp(-g)))


def _pool_kernel(x_ref, g_ref, w_ref, sc_ref, o_ref, l0, l1, l2, l3, *, ts, gw):
    si = pl.program_id(1)
    levels = (l0, l1, l2, l3)
    d = x_ref.shape[-1]

    @pl.when(si == 0)
    def _():
        for lv in levels:
            lv[0:POOL_HALO, :] = jnp.zeros((POOL_HALO, d), F32)

    x = x_ref[0]
    h = _rmsnorm(x, g_ref[...])
    l0[POOL_HALO:POOL_HALO + ts, :] = h
    for k in range(3):
        sh = 1 << k
        c0 = k * gw
        src, dst = levels[k], levels[k + 1]
        dst[POOL_HALO:POOL_HALO + ts, c0:] = (
            src[POOL_HALO:POOL_HALO + ts, c0:] + src[POOL_HALO - sh:POOL_HALO - sh + ts, c0:])
    c3 = 3 * gw
    s16 = l3[POOL_HALO:POOL_HALO + ts, c3:] + l3[POOL_HALO - 8:POOL_HALO - 8 + ts, c3:]

    t = si * ts + lax.broadcasted_iota(jnp.int32, (ts, 1), 0)
    for gi, w in enumerate(POOL_WINDOWS):
        sl = slice(gi * gw, (gi + 1) * gw)
        if gi < 3:
            s = levels[gi + 1][POOL_HALO:POOL_HALO + ts, sl]
        else:
            s = s16
        cnt = jnp.minimum(t + 1, w).astype(F32)
        p = (s / cnt - h[:, sl]).astype(BF16)
        y = _dot(p, w_ref[gi])
        o_ref[0, :, sl] = x[:, sl] + y * sc_ref[:, sl]

    for lv in levels:
        lv[0:POOL_HALO, :] = lv[ts:ts + POOL_HALO, :]


def _pool_layer(x, g, w_bf, scale, *, ts):
    b, s, d = x.shape
    ng, gw, _ = w_bf.shape
    ts = min(ts, s)
    kern = functools.partial(_pool_kernel, ts=ts, gw=gw)
    return pl.pallas_call(
        kern,
        out_shape=jax.ShapeDtypeStruct((b, s, d), F32),
        grid=(b, s // ts),
        in_specs=[
            pl.BlockSpec((1, ts, d), lambda bi, si: (bi, si, 0)),
            pl.BlockSpec((1, d), lambda bi, si: (0, 0)),
            pl.BlockSpec((ng, gw, gw), lambda bi, si: (0, 0, 0)),
            pl.BlockSpec((1, d), lambda bi, si: (0, 0)),
        ],
        out_specs=pl.BlockSpec((1, ts, d), lambda bi, si: (bi, si, 0)),
        scratch_shapes=[pltpu.VMEM((ts + POOL_HALO, d), F32) for _ in range(4)],
        compiler_params=_cparams(("arbitrary", "arbitrary")),
        name="pool_layer",
    )(x, g.reshape(1, d), w_bf, scale.reshape(1, d))


def _swiglu_accumulate(h, wg_ref, wu_ref, wo_ref, o_ref, between=None):
    tf = wo_ref.shape[0]
    for ci, c0 in enumerate(range(0, tf, FFN_CHUNK)):
        cs = slice(c0, min(c0 + FFN_CHUNK, tf))
        a = _silu(_dot(h, wg_ref[:, cs])) * _dot(h, wu_ref[:, cs])
        o_ref[...] += _dot(a.astype(BF16), wo_ref[cs, :])
        if between is not None:
            between(ci)


def _ffn_kernel(x_ref, g_ref, wg_ref, wu_ref, wo_ref, o_ref, h_sc):
    @pl.when(pl.program_id(1) == 0)
    def _():
        h_sc[...] = _rmsnorm(x_ref[...], g_ref[...]).astype(BF16)
        o_ref[...] = x_ref[...]

    _swiglu_accumulate(h_sc[...], wg_ref, wu_ref, wo_ref, o_ref)


def _dense_ffn(x2, g, w_in_bf, w_out_bf, *, tm, tf):
    n, d = x2.shape
    f = w_out_bf.shape[0]
    tm = min(tm, n)
    tf = min(tf, f)
    nj = f // tf
    return pl.pallas_call(
        _ffn_kernel,
        out_shape=jax.ShapeDtypeStruct((n, d), F32),
        grid=(n // tm, nj),
        in_specs=[
            pl.BlockSpec((tm, d), lambda i, j: (i, 0)),
            pl.BlockSpec((1, d), lambda i, j: (0, 0)),
            pl.BlockSpec((d, tf), lambda i, j: (0, j)),
            pl.BlockSpec((d, tf), lambda i, j: (0, j + nj)),
            pl.BlockSpec((tf, d), lambda i, j: (j, 0)),
        ],
        out_specs=pl.BlockSpec((tm, d), lambda i, j: (i, 0)),
        scratch_shapes=[pltpu.VMEM((tm, d), BF16)],
        compiler_params=_cparams(("arbitrary", "arbitrary")),
        name="dense_ffn",
    )(x2, g.reshape(1, d), w_in_bf, w_in_bf, w_out_bf)


def _qproj_kernel(x_ref, g_ref, w_ref, o_ref, *, out_scale):
    h = _rmsnorm(x_ref[0], g_ref[...]).astype(BF16)
    y = _dot(h, w_ref[...]) * out_scale
    o_ref[0] = y.T.astype(BF16)


def _q_proj(x, g, w_bf, *, tm, out_scale):
    b, s, d = x.shape
    tm = min(tm, s)
    return pl.pallas_call(
        functools.partial(_qproj_kernel, out_scale=out_scale),
        out_shape=jax.ShapeDtypeStruct((b, d, s), BF16),
        grid=(b, s // tm),
        in_specs=[
            pl.BlockSpec((1, tm, d), lambda bi, si: (bi, si, 0)),
            pl.BlockSpec((1, d), lambda bi, si: (0, 0)),
            pl.BlockSpec((d, d), lambda bi, si: (0, 0)),
        ],
        out_specs=pl.BlockSpec((1, d, tm), lambda bi, si: (bi, 0, si)),
        compiler_params=_cparams(("arbitrary", "arbitrary")),
        name="q_proj",
    )(x, g.reshape(1, d), w_bf)


def _kvproj_kernel(x_ref, g_ref, w_ref, k_ref, vt_ref):
    tm, d = x_ref.shape
    h = _rmsnorm(x_ref[...], g_ref[...]).astype(BF16)
    y = _dot(h, w_ref[...]).astype(BF16)
    row = lax.broadcasted_iota(jnp.int32, (KEY_BLOCK, KEY_BLOCK), 0)
    col = lax.broadcasted_iota(jnp.int32, (KEY_BLOCK, KEY_BLOCK), 1)
    perm = (col == KEY_CHUNK * (row % SUBLANES) + row // SUBLANES).astype(BF16)
    for blk in range(tm // KEY_BLOCK):
        rows = slice(blk * KEY_BLOCK, (blk + 1) * KEY_BLOCK)
        p = _dot(perm, y[rows, :])
        k_ref[rows, :] = p[:, :d].astype(BF16)
        vt_ref[blk] = p[:, d:].T.astype(BF16)


def _kv_proj(x2, g, w_bf, *, tm):
    n, d = x2.shape
    tm = max(min(tm, n), KEY_BLOCK)
    nb = tm // KEY_BLOCK
    return pl.pallas_call(
        _kvproj_kernel,
        out_shape=(jax.ShapeDtypeStruct((n, d), BF16),
                   jax.ShapeDtypeStruct((n // KEY_BLOCK, d, KEY_BLOCK), BF16)),
        grid=(n // tm,),
        in_specs=[
            pl.BlockSpec((tm, d), lambda i: (i, 0)),
            pl.BlockSpec((1, d), lambda i: (0, 0)),
            pl.BlockSpec((d, 2 * d), lambda i: (0, 0)),
        ],
        out_specs=(pl.BlockSpec((tm, d), lambda i: (i, 0)),
                   pl.BlockSpec((nb, d, KEY_BLOCK), lambda i: (i, 0, 0))),
        compiler_params=_cparams(("arbitrary",)),
        name="kv_proj",
    )(x2, g.reshape(1, d), w_bf)


def _oproj_kernel(x_ref, ot_ref, w_ref, o_ref):
    o = ot_ref[0].astype(F32).T.astype(BF16)
    o_ref[0] = x_ref[0] + _dot(o, w_ref[...])


def _out_proj(x, ot_bf, w_bf, *, tm):
    b, s, d = x.shape
    tm = min(tm, s)
    return pl.pallas_call(
        _oproj_kernel,
        out_shape=jax.ShapeDtypeStruct((b, s, d), F32),
        grid=(b, s // tm),
        in_specs=[
            pl.BlockSpec((1, tm, d), lambda bi, si: (bi, si, 0)),
            pl.BlockSpec((1, d, tm), lambda bi, si: (bi, 0, si)),
            pl.BlockSpec((d, d), lambda bi, si: (0, 0)),
        ],
        out_specs=pl.BlockSpec((1, tm, d), lambda bi, si: (bi, si, 0)),
        compiler_params=_cparams(("arbitrary", "arbitrary")),
        name="out_proj",
    )(x, ot_bf, w_bf)


def _attn_kernel(qt_ref, kp_ref, vt_ref, o_ref, z_sc, zd_sc, d_sc, at_sc, acc_sc):
    kb = KEY_BLOCK
    tq = KEY_BLOCK
    w = 2 * tq
    nr = kb // SUBLANES
    pw = 2 * HEAD_DIM
    n_pairs = qt_ref.shape[1] // pw
    s = qt_ref.shape[-1]
    sub = lax.broadcasted_iota(jnp.int32, (SUBLANES, w), 0)
    q_off = lax.broadcasted_iota(jnp.int32, (SUBLANES, w), 1) % tq
    head_a = lax.broadcasted_iota(jnp.int32, (pw, tq), 0) < HEAD_DIM

    def step(j, cs, qts, qi, diag, qts_next=None):
        slot = j % 2
        if diag:
            acc_sc[...] = jnp.zeros_like(acc_sc)
        else:
            for p in range(n_pairs):
                acc_sc[p] += _dot(vt_ref[0, j + 1, p * pw:(p + 1) * pw, :], at_sc[p])
        runs = []
        for p in range(n_pairs):
            zs = zd_sc.at[p] if diag else z_sc.at[slot, p]
            run = jnp.ones((SUBLANES, w), F32)
            for r in reversed(range(nr)):
                sl = slice(r * SUBLANES, (r + 1) * SUBLANES)
                t = jnp.tanh(zs[sl, :]) * 0.5
                if diag:
                    t = jnp.where((r + KEY_CHUNK * sub) < q_off, t, -0.5)
                d_sc[p, sl, :] = (0.5 + t) * run
                run = run * (0.5 - t)
            runs.append(run)
        if not diag or qi > 0:
            jk = jnp.maximum(j - 1, 0)
            for p in range(n_pairs):
                z_sc[1 - slot, p] = _dot(kp_ref[0, jk, :, p * pw:(p + 1) * pw], qts[p])
        if diag and qts_next is not None:
            for p in range(n_pairs):
                zd_sc[p] = _dot(kp_ref[0, qi + 1, :, p * pw:(p + 1) * pw], qts_next[p])
        out = []
        for p in range(n_pairs):
            run = runs[p]
            y = jnp.where(sub + 1 < SUBLANES, pltpu.roll(run, SUBLANES - 1, axis=0), 1.0)
            for dd in (1, 2, 4):
                y = y * jnp.where(sub + dd < SUBLANES, pltpu.roll(y, SUBLANES - dd, axis=0), 1.0)
            oc = y * cs[p]
            for r in range(0, nr, 2):
                two = [d_sc[p, rr * SUBLANES:(rr + 1) * SUBLANES, :] * oc for rr in (r, r + 1)]
                at_sc[p, r * SUBLANES:(r + 2) * SUBLANES, :] = jnp.concatenate(two, axis=0).astype(BF16)
            out.append(cs[p] * (y[0:1, :] * run[0:1, :]))
        return tuple(out)

    def q_operands(qi):
        out = []
        for p in range(n_pairs):
            q2 = qt_ref[0, p * pw:(p + 1) * pw, qi * tq:(qi + 1) * tq]
            zero = jnp.zeros_like(q2)
            out.append(jnp.concatenate([jnp.where(head_a, q2, zero), jnp.where(head_a, zero, q2)], axis=1))
        return out

    n_q = s // tq
    qts = q_operands(0)
    for p in range(n_pairs):
        zd_sc[p] = _dot(kp_ref[0, 0, :, p * pw:(p + 1) * pw], qts[p])
    for qi in range(n_q):
        cols = slice(qi * tq, (qi + 1) * tq)
        qts_next = q_operands(qi + 1) if qi + 1 < n_q else None
        ones = tuple(jnp.ones((1, w), F32) for _ in range(n_pairs))
        cs = step(qi, ones, qts, qi, True, qts_next)
        if qi > 0:
            cs = lax.fori_loop(0, qi, lambda it, cc: step(qi - 1 - it, cc, qts, qi, False), cs)
        qts = qts_next
        for p in range(n_pairs):
            acc = acc_sc[p] + _dot(vt_ref[0, 0, p * pw:(p + 1) * pw, :], at_sc[p])
            o_ref[0, p * pw:p * pw + HEAD_DIM, cols] = acc[:HEAD_DIM, :tq].astype(BF16)
            o_ref[0, p * pw + HEAD_DIM:(p + 1) * pw, cols] = acc[HEAD_DIM:, tq:].astype(BF16)


def _attention(qt, kp, vt):
    b, d, s = qt.shape
    nkb = s // KEY_BLOCK
    gw = ATTN_PAIRS * 2 * HEAD_DIM
    w = 2 * KEY_BLOCK
    return pl.pallas_call(
        _attn_kernel,
        out_shape=jax.ShapeDtypeStruct((b, d, s), BF16),
        grid=(b, d // gw),
        in_specs=[
            pl.BlockSpec((1, gw, s), lambda bi, gi: (bi, gi, 0)),
            pl.BlockSpec((1, nkb, KEY_BLOCK, gw), lambda bi, gi: (bi, 0, 0, gi)),
            pl.BlockSpec((1, nkb, gw, KEY_BLOCK), lambda bi, gi: (bi, 0, gi, 0)),
        ],
        out_specs=pl.BlockSpec((1, gw, s), lambda bi, gi: (bi, gi, 0)),
        scratch_shapes=[
            pltpu.VMEM((2, ATTN_PAIRS, KEY_BLOCK, w), F32),
            pltpu.VMEM((ATTN_PAIRS, KEY_BLOCK, w), F32),
            pltpu.VMEM((ATTN_PAIRS, KEY_BLOCK, w), F32),
            pltpu.VMEM((ATTN_PAIRS, KEY_BLOCK, w), BF16),
            pltpu.VMEM((ATTN_PAIRS, 2 * HEAD_DIM, w), F32),
        ],
        compiler_params=_cparams(("arbitrary", "arbitrary")),
        name="stickbreak_attn",
    )(qt, kp, vt)


def _router_kernel(x_ref, g_ref, wr_ref, h_ref, meta_ref, gate_ref, cnt_ref, run_sc, *, n_exp):
    i = pl.program_id(0)
    tm = x_ref.shape[0]

    @pl.when(i == 0)
    def _():
        run_sc[...] = jnp.zeros_like(run_sc)

    h = _rmsnorm(x_ref[...], g_ref[...])
    h_ref[...] = h
    logits = jnp.dot(h, wr_ref[...], preferred_element_type=F32, precision=lax.Precision.HIGHEST)
    lane = lax.broadcasted_iota(jnp.int32, (tm, LANES), 1)
    neg = jnp.float32(-jnp.inf)
    logits = jnp.where(lane < n_exp, logits, neg)
    m1 = jnp.max(logits, axis=-1, keepdims=True)
    i1 = jnp.min(jnp.where(logits == m1, lane, LANES), axis=-1, keepdims=True)
    rest = jnp.where(lane == i1, neg, logits)
    m2 = jnp.max(rest, axis=-1, keepdims=True)
    i2 = jnp.min(jnp.where(rest == m2, lane, LANES), axis=-1, keepdims=True)
    e = jnp.exp(m2 - m1)
    den = 1.0 + e
    g1 = 1.0 / den
    g2 = e / den

    sel1 = lane == i1
    sel2 = lane == i2
    onehot = jnp.where(sel1 | sel2, 1.0, 0.0)
    tri = (lax.broadcasted_iota(jnp.int32, (tm, tm), 1)
           < lax.broadcasted_iota(jnp.int32, (tm, tm), 0)).astype(BF16)
    before = _dot(tri, onehot.astype(BF16)) + run_sc[...]
    r1 = jnp.sum(jnp.where(sel1, before, 0.0), axis=-1, keepdims=True).astype(jnp.int32)
    r2 = jnp.sum(jnp.where(sel2, before, 0.0), axis=-1, keepdims=True).astype(jnp.int32)
    run_sc[...] += jnp.sum(onehot, axis=0, keepdims=True)

    meta = jnp.where(lane == 0, i1, jnp.where(lane == 1, i2, jnp.where(lane == 2, r1, jnp.where(lane == 3, r2, 0))))
    gates = jnp.where(lane == 0, g1, jnp.where(lane == 1, g2, 0.0))
    meta_ref[...] = meta[:, :SUBLANES]
    gate_ref[...] = gates[:, :SUBLANES]
    cnt_ref[...] = run_sc[...].astype(jnp.int32)


def _router(x2, g, w_router, *, tm):
    n, d = x2.shape
    n_exp = w_router.shape[1]
    tm = min(tm, n)
    wr = jnp.zeros((d, LANES), F32).at[:, :n_exp].set(w_router)
    return pl.pallas_call(
        functools.partial(_router_kernel, n_exp=n_exp),
        out_shape=(
            jax.ShapeDtypeStruct((n, d), F32),
            jax.ShapeDtypeStruct((n, SUBLANES), jnp.int32),
            jax.ShapeDtypeStruct((n, SUBLANES), F32),
            jax.ShapeDtypeStruct((1, LANES), jnp.int32),
        ),
        grid=(n // tm,),
        in_specs=[
            pl.BlockSpec((tm, d), lambda i: (i, 0)),
            pl.BlockSpec((1, d), lambda i: (0, 0)),
            pl.BlockSpec((d, LANES), lambda i: (0, 0)),
        ],
        out_specs=(
            pl.BlockSpec((tm, d), lambda i: (i, 0)),
            pl.BlockSpec((tm, SUBLANES), lambda i: (i, 0)),
            pl.BlockSpec((tm, SUBLANES), lambda i: (i, 0)),
            pl.BlockSpec((1, LANES), lambda i: (0, 0)),
        ),
        scratch_shapes=[pltpu.VMEM((1, LANES), F32)],
        compiler_params=_cparams(("arbitrary",)),
        name="moe_router",
    )(x2, g.reshape(1, d), wr)


def _expert_kernel(te_ref, nu_ref, first_ref, nxt_ref, prv_ref, h_ref, wg_ref, wu_ref, wo_ref, y2_ref,
                   xbuf, xb_sc, ybuf, sem_g, sem_s):
    i = pl.program_id(0)
    j = pl.program_id(1)
    n_i = pl.num_programs(0)
    n_j = pl.num_programs(1)
    tm, d = xb_sc.shape
    per_step = tm // n_j
    cur = i % 2
    oth = 1 - cur
    used = i < nu_ref[0]

    def gather(tok_ref, r, slot):
        return pltpu.make_async_copy(
            h_ref.at[pl.ds(tok_ref[r], 1), :], xbuf.at[slot, pl.ds(r, 1), :], sem_g.at[slot])

    def scatter(dst_ref, r, slot):
        return pltpu.make_async_copy(
            ybuf.at[slot, pl.ds(r, 1), :], y2_ref.at[pl.ds(dst_ref[r], 1), :], sem_s.at[slot])

    def wait_gathers(slot):
        pltpu.make_async_copy(h_ref.at[pl.ds(0, tm), :], xbuf.at[slot], sem_g.at[slot]).wait()

    def wait_scatters(slot):
        pltpu.make_async_copy(ybuf.at[slot], y2_ref.at[pl.ds(0, tm), :], sem_s.at[slot]).wait()

    def issue(rr):
        r = j * per_step + rr
        gather(nxt_ref, r, oth).start()
        scatter(prv_ref, r, oth).start()

    @pl.when((i == 0) & (j == 0))
    def _():
        ybuf[...] = jnp.zeros_like(ybuf)

        def first(r, carry):
            gather(first_ref, r, 0).start()
            return carry

        lax.fori_loop(0, tm, first, 0)

    @pl.when(j == 0)
    def _():
        wait_gathers(cur)

        @pl.when(i > 0)
        def _():
            wait_scatters(cur)

        xb_sc[...] = xbuf[cur].astype(BF16)
        ybuf[cur] = jnp.zeros((tm, d), F32)

    @pl.when(used)
    def _():
        n_chunks = pl.cdiv(wo_ref.shape[1], FFN_CHUNK)
        per_chunk = pl.cdiv(per_step, n_chunks)

        def between(ci):
            for rr in range(ci * per_chunk, min((ci + 1) * per_chunk, per_step)):
                issue(rr)

        _swiglu_accumulate(xb_sc[...], wg_ref.at[0], wu_ref.at[0], wo_ref.at[0], ybuf.at[cur], between)

    @pl.when(jnp.logical_not(used))
    def _():
        def body(rr, carry):
            issue(rr)
            return carry

        lax.fori_loop(0, per_step, body, 0)

    @pl.when((i == n_i - 1) & (j == n_j - 1))
    def _():
        wait_gathers(oth)
        wait_scatters(oth)


def _expert_ffn(h2, src_tok, dst_row, tile_e, n_used, w_in_bf, w_out_bf, *, tm, tf):
    n, d = h2.shape
    n_exp, f, _ = w_out_bf.shape
    tf = min(tf, f)
    nj = f // tf
    n_tiles = src_tok.shape[0] // tm - 2

    def jeff(i, j, nu):
        return jnp.where(i < nu[0], j, nj - 1)

    smem = functools.partial(pl.BlockSpec, (tm,), memory_space=pltpu.SMEM)
    return pl.pallas_call(
        _expert_kernel,
        out_shape=jax.ShapeDtypeStruct((TOP_K * (n + tm), d), F32),
        grid_spec=pltpu.PrefetchScalarGridSpec(
            num_scalar_prefetch=2,
            grid=(n_tiles, nj),
            in_specs=[
                smem(lambda i, j, te, nu: (1,)),
                smem(lambda i, j, te, nu: (i + 2,)),
                smem(lambda i, j, te, nu: (i,)),
                pl.BlockSpec(memory_space=pl.ANY),
                pl.BlockSpec((1, d, tf), lambda i, j, te, nu: (te[i], 0, jeff(i, j, nu))),
                pl.BlockSpec((1, d, tf), lambda i, j, te, nu: (te[i], 0, jeff(i, j, nu) + nj)),
                pl.BlockSpec((1, tf, d), lambda i, j, te, nu: (te[i], jeff(i, j, nu), 0)),
            ],
            out_specs=pl.BlockSpec(memory_space=pl.ANY),
            scratch_shapes=[
                pltpu.VMEM((2, tm, d), F32),
                pltpu.VMEM((tm, d), BF16),
                pltpu.VMEM((2, tm, d), F32),
                pltpu.SemaphoreType.DMA((2,)),
                pltpu.SemaphoreType.DMA((2,)),
            ],
        ),
        compiler_params=_cparams(("arbitrary", "arbitrary")),
        name="moe_expert_ffn",
    )(tile_e, n_used, src_tok, src_tok, dst_row, h2, w_in_bf, w_in_bf, w_out_bf)


def _combine_kernel(x_ref, gate_ref, gf_ref, y2_ref, o_ref, *, final_norm):
    g = gate_ref[...]
    out = x_ref[...] + (y2_ref[0] * g[:, 0:1] + y2_ref[1] * g[:, 1:2])
    if final_norm:
        out = _rmsnorm(out, gf_ref[...])
    o_ref[...] = out


def _combine(x2, gates, y2, g_final, *, tm, final_norm):
    n, d = x2.shape
    tm = min(tm, n)
    return pl.pallas_call(
        functools.partial(_combine_kernel, final_norm=final_norm),
        out_shape=jax.ShapeDtypeStruct((n, d), F32),
        grid=(n // tm,),
        in_specs=[
            pl.BlockSpec((tm, d), lambda i: (i, 0)),
            pl.BlockSpec((tm, SUBLANES), lambda i: (i, 0)),
            pl.BlockSpec((1, d), lambda i: (0, 0)),
            pl.BlockSpec((TOP_K, tm, d), lambda i: (0, i, 0)),
        ],
        out_specs=pl.BlockSpec((tm, d), lambda i: (i, 0)),
        compiler_params=_cparams(("arbitrary",)),
        name="moe_combine",
    )(x2, gates, g_final.reshape(1, d), y2.reshape(TOP_K, -1, d))


def _moe_layer(x2, g, w_router, w_in_bf, w_out_bf, g_final, *, final_norm, tm, tm_e, tf):
    n, d = x2.shape
    n_exp = w_router.shape[1]
    tm_e = min(tm_e, n)
    h, meta, gates, cnt = _router(x2, g, w_router, tm=tm)
    counts = cnt[0, :n_exp]
    padded = (counts + tm_e - 1) // tm_e * tm_e
    pend = jnp.cumsum(padded)
    pstart = pend - padded
    dest = (pstart[meta[:, 0:TOP_K]] + meta[:, TOP_K:2 * TOP_K]).reshape(-1).astype(jnp.int32)
    cap = n * TOP_K + n_exp * tm_e
    n_tiles = cap // tm_e
    n_used = (pend[-1] // tm_e).astype(jnp.int32)
    tile_start = jnp.arange(n_tiles, dtype=jnp.int32) * tm_e
    tile_e = jnp.minimum(jnp.searchsorted(pend, tile_start, side="right"), n_exp - 1).astype(jnp.int32)
    tile_e = jnp.where(jnp.arange(n_tiles) < n_used, tile_e, tile_e[n_used - 1])
    rows = jnp.arange(cap + 2 * tm_e, dtype=jnp.int32)
    slots = jnp.full((cap + 2 * tm_e,), -1, jnp.int32).at[dest + tm_e].set(
        jnp.arange(n * TOP_K, dtype=jnp.int32), unique_indices=True, mode="promise_in_bounds")
    src_tok = jnp.maximum(slots, 0) // TOP_K
    plane = n + tm_e
    trash = (rows // tm_e % 2) * plane + n + rows % tm_e
    dst_row = jnp.where(slots < 0, trash, slots % TOP_K * plane + slots // TOP_K)

    y2 = _expert_ffn(h, src_tok, dst_row, tile_e, n_used.reshape(1), w_in_bf, w_out_bf, tm=tm_e, tf=tf)
    return _combine(x2, gates, y2, g_final, tm=tm, final_norm=final_norm)


def _trunk(x, g_mix, g_ffn, pool_w, pool_scale, g_kv, w_kv, w_q, w_o, ffn_in, ffn_out,
           w_router, moe_in, moe_out, g_final, *, ts, tm, tf_dense, tm_e, tf_moe):
    b, s, d = x.shape
    n = b * s
    depth = g_mix.shape[0]
    n_a = pool_w.shape[0]
    nh = d // HEAD_DIM
    kp = vt = None
    for l in range(depth):
        if l < n_a:
            x = _pool_layer(x, g_mix[l], pool_w[l].astype(BF16), pool_scale[l], ts=ts)
        else:
            bi = l - n_a
            qt = _q_proj(x, g_mix[l], w_q[bi].astype(BF16), tm=tm, out_scale=0.5 * HEAD_DIM ** -0.5)
            ot = _attention(qt, kp, vt)
            x = _out_proj(x, ot, w_o[bi].astype(BF16), tm=tm)
        x2 = x.reshape(n, d)
        if l % 2 == 0:
            x2 = _dense_ffn(x2, g_ffn[l], ffn_in[l // 2].astype(BF16), ffn_out[l // 2].astype(BF16),
                            tm=tm, tf=tf_dense)
        else:
            x2 = _moe_layer(x2, g_ffn[l], w_router[l // 2], moe_in[l // 2].astype(BF16),
                            moe_out[l // 2].astype(BF16), g_final,
                            final_norm=(l == depth - 1), tm=tm, tm_e=tm_e, tf=tf_moe)
        x = x2.reshape(b, s, d)
        if l == n_a - 1:
            kp, vt = _kv_proj(x2, g_kv, w_kv.astype(BF16), tm=tm)
            kp = kp.reshape(b, s // KEY_BLOCK, KEY_BLOCK, d)
            vt = vt.reshape(b, s // KEY_BLOCK, d, KEY_BLOCK)
    return x


def kernel(x, g_mix, g_ffn, pool_w, pool_scale, g_kv, w_kv, w_q, w_o, ffn_in, ffn_out,
           w_router, moe_in, moe_out, g_final):
    return _trunk(x, g_mix, g_ffn, pool_w, pool_scale, g_kv, w_kv, w_q, w_o, ffn_in, ffn_out,
                  w_router, moe_in, moe_out, g_final,
                  ts=512, tm=512, tf_dense=2816, tm_e=1024, tf_moe=1792)
```

```python
import functools

import jax
import jax.numpy as jnp
from jax import lax
from jax.experimental import pallas as pl
from jax.experimental.pallas import tpu as pltpu

RMS_EPS = 1e-6
POOL_WINDOWS = (2, 4, 8, 16)
POOL_HALO = 16
HEAD_DIM = 64
TOP_K = 2
LANES = 128
SUBLANES = 8
KEY_BLOCK = 256
KEY_CHUNK = KEY_BLOCK // SUBLANES
ATTN_PAIRS = 2
FFN_CHUNK = 256
VMEM_LIMIT = 56 * 1024 * 1024

F32 = jnp.float32
BF16 = jnp.bfloat16


def _cparams(sem):
    return pltpu.CompilerParams(dimension_semantics=sem, vmem_limit_bytes=VMEM_LIMIT)


def _rmsnorm(x, g):
    inv = lax.rsqrt(jnp.mean(x * x, axis=-1, keepdims=True) + RMS_EPS)
    return (x * inv) * g


def _dot(a, b):
    return jnp.dot(a, b, preferred_element_type=F32)


def _silu(g):
    return g * (1.0 / (1.0 + jnp.exp(-g)))


def _pool_kernel(x_ref, g_ref, w_ref, sc_ref, o_ref, l0, l1, l2, l3, *, ts, gw):
    si = pl.program_id(1)
    levels = (l0, l1, l2, l3)
    d = x_ref.shape[-1]

    @pl.when(si == 0)
    def _():
        for lv in levels:
            lv[0:POOL_HALO, :] = jnp.zeros((POOL_HALO, d), F32)

    x = x_ref[0]
    h = _rmsnorm(x, g_ref[...])
    l0[POOL_HALO:POOL_HALO + ts, :] = h
    for k in range(3):
        sh = 1 << k
        c0 = k * gw
        src, dst = levels[k], levels[k + 1]
        dst[POOL_HALO:POOL_HALO + ts, c0:] = (
            src[POOL_HALO:POOL_HALO + ts, c0:] + src[POOL_HALO - sh:POOL_HALO - sh + ts, c0:])
    c3 = 3 * gw
    s16 = l3[POOL_HALO:POOL_HALO + ts, c3:] + l3[POOL_HALO - 8:POOL_HALO - 8 + ts, c3:]

    t = si * ts + lax.broadcasted_iota(jnp.int32, (ts, 1), 0)
    for gi, w in enumerate(POOL_WINDOWS):
        sl = slice(gi * gw, (gi + 1) * gw)
        if gi < 3:
            s = levels[gi + 1][POOL_HALO:POOL_HALO + ts, sl]
        else:
            s = s16
        cnt = jnp.minimum(t + 1, w).astype(F32)
        p = (s / cnt - h[:, sl]).astype(BF16)
        y = _dot(p, w_ref[gi])
        o_ref[0, :, sl] = x[:, sl] + y * sc_ref[:, sl]

    for lv in levels:
        lv[0:POOL_HALO, :] = lv[ts:ts + POOL_HALO, :]


def _pool_layer(x, g, w_bf, scale, *, ts):
    b, s, d = x.shape
    ng, gw, _ = w_bf.shape
    ts = min(ts, s)
    kern = functools.partial(_pool_kernel, ts=ts, gw=gw)
    return pl.pallas_call(
        kern,
        out_shape=jax.ShapeDtypeStruct((b, s, d), F32),
        grid=(b, s // ts),
        in_specs=[
            pl.BlockSpec((1, ts, d), lambda bi, si: (bi, si, 0)),
            pl.BlockSpec((1, d), lambda bi, si: (0, 0)),
            pl.BlockSpec((ng, gw, gw), lambda bi, si: (0, 0, 0)),
            pl.BlockSpec((1, d), lambda bi, si: (0, 0)),
        ],
        out_specs=pl.BlockSpec((1, ts, d), lambda bi, si: (bi, si, 0)),
        scratch_shapes=[pltpu.VMEM((ts + POOL_HALO, d), F32) for _ in range(4)],
        compiler_params=_cparams(("arbitrary", "arbitrary")),
        name="pool_layer",
    )(x, g.reshape(1, d), w_bf, scale.reshape(1, d))


def _swiglu_accumulate(h, wg_ref, wu_ref, wo_ref, o_ref, between=None):
    tf = wo_ref.shape[0]
    emit = between if between is not None else (lambda k: None)
    for ci, c0 in enumerate(range(0, tf, FFN_CHUNK)):
        cs = slice(c0, min(c0 + FFN_CHUNK, tf))
        gate = _dot(h, wg_ref[:, cs])
        emit(3 * ci)
        up = _dot(h, wu_ref[:, cs])
        emit(3 * ci + 1)
        o_ref[...] += _dot((_silu(gate) * up).astype(BF16), wo_ref[cs, :])
        emit(3 * ci + 2)


def _ffn_kernel(x_ref, g_ref, wg_ref, wu_ref, wo_ref, o_ref, h_sc):
    @pl.when(pl.program_id(1) == 0)
    def _():
        h_sc[...] = _rmsnorm(x_ref[...], g_ref[...]).astype(BF16)
        o_ref[...] = x_ref[...]

    _swiglu_accumulate(h_sc[...], wg_ref, wu_ref, wo_ref, o_ref)


def _dense_ffn(x2, g, w_in_bf, w_out_bf, *, tm, tf):
    n, d = x2.shape
    f = w_out_bf.shape[0]
    tm = min(tm, n)
    tf = min(tf, f)
    nj = f // tf
    return pl.pallas_call(
        _ffn_kernel,
        out_shape=jax.ShapeDtypeStruct((n, d), F32),
        grid=(n // tm, nj),
        in_specs=[
            pl.BlockSpec((tm, d), lambda i, j: (i, 0)),
            pl.BlockSpec((1, d), lambda i, j: (0, 0)),
            pl.BlockSpec((d, tf), lambda i, j: (0, j)),
            pl.BlockSpec((d, tf), lambda i, j: (0, j + nj)),
            pl.BlockSpec((tf, d), lambda i, j: (j, 0)),
        ],
        out_specs=pl.BlockSpec((tm, d), lambda i, j: (i, 0)),
        scratch_shapes=[pltpu.VMEM((tm, d), BF16)],
        compiler_params=_cparams(("arbitrary", "arbitrary")),
        name="dense_ffn",
    )(x2, g.reshape(1, d), w_in_bf, w_in_bf, w_out_bf)


def _qproj_kernel(x_ref, g_ref, w_ref, o_ref, *, out_scale):
    h = _rmsnorm(x_ref[0], g_ref[...]).astype(BF16)
    y = _dot(h, w_ref[...]) * out_scale
    o_ref[0] = y.T.astype(BF16)


def _q_proj(x, g, w_bf, *, tm, out_scale):
    b, s, d = x.shape
    tm = min(tm, s)
    return pl.pallas_call(
        functools.partial(_qproj_kernel, out_scale=out_scale),
        out_shape=jax.ShapeDtypeStruct((b, d, s), BF16),
        grid=(b, s // tm),
        in_specs=[
            pl.BlockSpec((1, tm, d), lambda bi, si: (bi, si, 0)),
            pl.BlockSpec((1, d), lambda bi, si: (0, 0)),
            pl.BlockSpec((d, d), lambda bi, si: (0, 0)),
        ],
        out_specs=pl.BlockSpec((1, d, tm), lambda bi, si: (bi, 0, si)),
        compiler_params=_cparams(("arbitrary", "arbitrary")),
        name="q_proj",
    )(x, g.reshape(1, d), w_bf)


def _kvproj_kernel(x_ref, g_ref, w_ref, k_ref, vt_ref):
    tm, d = x_ref.shape
    h = _rmsnorm(x_ref[...], g_ref[...]).astype(BF16)
    y = _dot(h, w_ref[...]).astype(BF16)
    row = lax.broadcasted_iota(jnp.int32, (KEY_BLOCK, KEY_BLOCK), 0)
    col = lax.broadcasted_iota(jnp.int32, (KEY_BLOCK, KEY_BLOCK), 1)
    perm = (col == KEY_CHUNK * (row % SUBLANES) + row // SUBLANES).astype(BF16)
    for blk in range(tm // KEY_BLOCK):
        rows = slice(blk * KEY_BLOCK, (blk + 1) * KEY_BLOCK)
        p = _dot(perm, y[rows, :])
        k_ref[rows, :] = p[:, :d].astype(BF16)
        vt_ref[blk] = p[:, d:].T.astype(BF16)


def _kv_proj(x2, g, w_bf, *, tm):
    n, d = x2.shape
    tm = max(min(tm, n), KEY_BLOCK)
    nb = tm // KEY_BLOCK
    return pl.pallas_call(
        _kvproj_kernel,
        out_shape=(jax.ShapeDtypeStruct((n, d), BF16),
                   jax.ShapeDtypeStruct((n // KEY_BLOCK, d, KEY_BLOCK), BF16)),
        grid=(n // tm,),
        in_specs=[
            pl.BlockSpec((tm, d), lambda i: (i, 0)),
            pl.BlockSpec((1, d), lambda i: (0, 0)),
            pl.BlockSpec((d, 2 * d), lambda i: (0, 0)),
        ],
        out_specs=(pl.BlockSpec((tm, d), lambda i: (i, 0)),
                   pl.BlockSpec((nb, d, KEY_BLOCK), lambda i: (i, 0, 0))),
        compiler_params=_cparams(("arbitrary",)),
        name="kv_proj",
    )(x2, g.reshape(1, d), w_bf)


def _oproj_kernel(x_ref, ot_ref, w_ref, o_ref):
    o = ot_ref[0].astype(F32).T.astype(BF16)
    o_ref[0] = x_ref[0] + _dot(o, w_ref[...])


def _out_proj(x, ot_bf, w_bf, *, tm):
    b, s, d = x.shape
    tm = min(tm, s)
    return pl.pallas_call(
        _oproj_kernel,
        out_shape=jax.ShapeDtypeStruct((b, s, d), F32),
        grid=(b, s // tm),
        in_specs=[
            pl.BlockSpec((1, tm, d), lambda bi, si: (bi, si, 0)),
            pl.BlockSpec((1, d, tm), lambda bi, si: (bi, 0, si)),
            pl.BlockSpec((d, d), lambda bi, si: (0, 0)),
        ],
        out_specs=pl.BlockSpec((1, tm, d), lambda bi, si: (bi, si, 0)),
        compiler_params=_cparams(("arbitrary", "arbitrary")),
        name="out_proj",
    )(x, ot_bf, w_bf)


def _attn_kernel(qt_ref, kp_ref, vt_ref, o_ref, z_sc, zd_sc, d_sc, at_sc, acc_sc):
    kb = KEY_BLOCK
    tq = KEY_BLOCK
    w = 2 * tq
    nr = kb // SUBLANES
    pw = 2 * HEAD_DIM
    n_pairs = qt_ref.shape[1] // pw
    s = qt_ref.shape[-1]
    sub = lax.broadcasted_iota(jnp.int32, (SUBLANES, w), 0)
    q_off = lax.broadcasted_iota(jnp.int32, (SUBLANES, w), 1) % tq
    head_a = lax.broadcasted_iota(jnp.int32, (pw, tq), 0) < HEAD_DIM

    def step(j, cs, qts, qi, diag, qts_next=None):
        slot = j % 2
        if diag:
            acc_sc[...] = jnp.zeros_like(acc_sc)
        else:
            for p in range(n_pairs):
                acc_sc[p] += _dot(vt_ref[0, j + 1, p * pw:(p + 1) * pw, :], at_sc[p])
        runs = []
        for p in range(n_pairs):
            zs = zd_sc.at[p] if diag else z_sc.at[slot, p]
            run = jnp.ones((SUBLANES, w), F32)
            for r in reversed(range(nr)):
                sl = slice(r * SUBLANES, (r + 1) * SUBLANES)
                t = jnp.tanh(zs[sl, :]) * 0.5
                if diag:
                    t = jnp.where((r + KEY_CHUNK * sub) < q_off, t, -0.5)
                d_sc[p, sl, :] = (0.5 + t) * run
                run = run * (0.5 - t)
            runs.append(run)
        if not diag or qi > 0:
            jk = jnp.maximum(j - 1, 0)
            for p in range(n_pairs):
                z_sc[1 - slot, p] = _dot(kp_ref[0, jk, :, p * pw:(p + 1) * pw], qts[p])
        if diag and qts_next is not None:
            for p in range(n_pairs):
                zd_sc[p] = _dot(kp_ref[0, qi + 1, :, p * pw:(p + 1) * pw], qts_next[p])
        out = []
        for p in range(n_pairs):
            run = runs[p]
            y = jnp.where(sub + 1 < SUBLANES, pltpu.roll(run, SUBLANES - 1, axis=0), 1.0)
            for dd in (1, 2, 4):
                y = y * jnp.where(sub + dd < SUBLANES, pltpu.roll(y, SUBLANES - dd, axis=0), 1.0)
            oc = y * cs[p]
            for r in range(0, nr, 2):
                two = [d_sc[p, rr * SUBLANES:(rr + 1) * SUBLANES, :] * oc for rr in (r, r + 1)]
                at_sc[p, r * SUBLANES:(r + 2) * SUBLANES, :] = jnp.concatenate(two, axis=0).astype(BF16)
            out.append(cs[p] * (y[0:1, :] * run[0:1, :]))
        return tuple(out)

    def q_operands(qi):
        out = []
        for p in range(n_pairs):
            q2 = qt_ref[0, p * pw:(p + 1) * pw, qi * tq:(qi + 1) * tq]
            zero = jnp.zeros_like(q2)
            out.append(jnp.concatenate([jnp.where(head_a, q2, zero), jnp.where(head_a, zero, q2)], axis=1))
        return out

    n_q = s // tq
    qts = q_operands(0)
    for p in range(n_pairs):
        zd_sc[p] = _dot(kp_ref[0, 0, :, p * pw:(p + 1) * pw], qts[p])
    for qi in range(n_q):
        cols = slice(qi * tq, (qi + 1) * tq)
        qts_next = q_operands(qi + 1) if qi + 1 < n_q else None
        ones = tuple(jnp.ones((1, w), F32) for _ in range(n_pairs))
        cs = step(qi, ones, qts, qi, True, qts_next)
        if qi > 0:
            cs = lax.fori_loop(0, qi, lambda it, cc: step(qi - 1 - it, cc, qts, qi, False), cs)
        qts = qts_next
        for p in range(n_pairs):
            acc = acc_sc[p] + _dot(vt_ref[0, 0, p * pw:(p + 1) * pw, :], at_sc[p])
            o_ref[0, p * pw:p * pw + HEAD_DIM, cols] = acc[:HEAD_DIM, :tq].astype(BF16)
            o_ref[0, p * pw + HEAD_DIM:(p + 1) * pw, cols] = acc[HEAD_DIM:, tq:].astype(BF16)


def _attention(qt, kp, vt):
    b, d, s = qt.shape
    nkb = s // KEY_BLOCK
    gw = ATTN_PAIRS * 2 * HEAD_DIM
    w = 2 * KEY_BLOCK
    return pl.pallas_call(
        _attn_kernel,
        out_shape=jax.ShapeDtypeStruct((b, d, s), BF16),
        grid=(b, d // gw),
        in_specs=[
            pl.BlockSpec((1, gw, s), lambda bi, gi: (bi, gi, 0)),
            pl.BlockSpec((1, nkb, KEY_BLOCK, gw), lambda bi, gi: (bi, 0, 0, gi)),
            pl.BlockSpec((1, nkb, gw, KEY_BLOCK), lambda bi, gi: (bi, 0, gi, 0)),
        ],
        out_specs=pl.BlockSpec((1, gw, s), lambda bi, gi: (bi, gi, 0)),
        scratch_shapes=[
            pltpu.VMEM((2, ATTN_PAIRS, KEY_BLOCK, w), F32),
            pltpu.VMEM((ATTN_PAIRS, KEY_BLOCK, w), F32),
            pltpu.VMEM((ATTN_PAIRS, KEY_BLOCK, w), F32),
            pltpu.VMEM((ATTN_PAIRS, KEY_BLOCK, w), BF16),
            pltpu.VMEM((ATTN_PAIRS, 2 * HEAD_DIM, w), F32),
        ],
        compiler_params=_cparams(("arbitrary", "arbitrary")),
        name="stickbreak_attn",
    )(qt, kp, vt)


def _router_kernel(x_ref, g_ref, wr_ref, h_ref, meta_ref, gate_ref, cnt_ref, run_sc, *, n_exp):
    i = pl.program_id(0)
    tm = x_ref.shape[0]

    @pl.when(i == 0)
    def _():
        run_sc[...] = jnp.zeros_like(run_sc)

    h = _rmsnorm(x_ref[...], g_ref[...])
    h_ref[...] = h
    w = wr_ref[...]
    h_hi = h.astype(BF16)
    h_lo = (h - h_hi.astype(F32)).astype(BF16)
    w_hi = w.astype(BF16)
    w_lo = (w - w_hi.astype(F32)).astype(BF16)
    logits = (_dot(h_hi, w_hi) + _dot(h_hi, w_lo)) + _dot(h_lo, w_hi)
    lane = lax.broadcasted_iota(jnp.int32, (tm, LANES), 1)
    neg = jnp.float32(-jnp.inf)
    logits = jnp.where(lane < n_exp, logits, neg)
    m1 = jnp.max(logits, axis=-1, keepdims=True)
    i1 = jnp.min(jnp.where(logits == m1, lane, LANES), axis=-1, keepdims=True)
    rest = jnp.where(lane == i1, neg, logits)
    m2 = jnp.max(rest, axis=-1, keepdims=True)
    i2 = jnp.min(jnp.where(rest == m2, lane, LANES), axis=-1, keepdims=True)
    e = jnp.exp(m2 - m1)
    den = 1.0 + e
    g1 = 1.0 / den
    g2 = e / den

    sel1 = lane == i1
    sel2 = lane == i2
    onehot = jnp.where(sel1 | sel2, 1.0, 0.0)
    tri = (lax.broadcasted_iota(jnp.int32, (tm, tm), 1)
           < lax.broadcasted_iota(jnp.int32, (tm, tm), 0)).astype(BF16)
    before = _dot(tri, onehot.astype(BF16)) + run_sc[...]
    r1 = jnp.sum(jnp.where(sel1, before, 0.0), axis=-1, keepdims=True).astype(jnp.int32)
    r2 = jnp.sum(jnp.where(sel2, before, 0.0), axis=-1, keepdims=True).astype(jnp.int32)
    run_sc[...] += jnp.sum(onehot, axis=0, keepdims=True)

    meta = jnp.where(lane == 0, i1, jnp.where(lane == 1, i2, jnp.where(lane == 2, r1, jnp.where(lane == 3, r2, 0))))
    gates = jnp.where(lane == 0, g1, jnp.where(lane == 1, g2, 0.0))
    meta_ref[...] = meta[:, :SUBLANES]
    gate_ref[...] = gates[:, :SUBLANES]
    cnt_ref[...] = run_sc[...].astype(jnp.int32)


def _router(x2, g, w_router, *, tm):
    n, d = x2.shape
    n_exp = w_router.shape[1]
    tm = min(tm, n)
    wr = jnp.zeros((d, LANES), F32).at[:, :n_exp].set(w_router)
    return pl.pallas_call(
        functools.partial(_router_kernel, n_exp=n_exp),
        out_shape=(
            jax.ShapeDtypeStruct((n, d), F32),
            jax.ShapeDtypeStruct((n, SUBLANES), jnp.int32),
            jax.ShapeDtypeStruct((n, SUBLANES), F32),
            jax.ShapeDtypeStruct((1, LANES), jnp.int32),
        ),
        grid=(n // tm,),
        in_specs=[
            pl.BlockSpec((tm, d), lambda i: (i, 0)),
            pl.BlockSpec((1, d), lambda i: (0, 0)),
            pl.BlockSpec((d, LANES), lambda i: (0, 0)),
        ],
        out_specs=(
            pl.BlockSpec((tm, d), lambda i: (i, 0)),
            pl.BlockSpec((tm, SUBLANES), lambda i: (i, 0)),
            pl.BlockSpec((tm, SUBLANES), lambda i: (i, 0)),
            pl.BlockSpec((1, LANES), lambda i: (0, 0)),
        ),
        scratch_shapes=[pltpu.VMEM((1, LANES), F32)],
        compiler_params=_cparams(("arbitrary",)),
        name="moe_router",
    )(x2, g.reshape(1, d), wr)


def _expert_kernel(te_ref, nu_ref, first_ref, nxt_ref, prv_ref, h_ref, wg_ref, wu_ref, wo_ref, y2_ref,
                   xbuf, xb_sc, ybuf, sem_g, sem_s):
    i = pl.program_id(0)
    j = pl.program_id(1)
    n_i = pl.num_programs(0)
    n_j = pl.num_programs(1)
    tm, d = xb_sc.shape
    per_step = tm // n_j
    cur = i % 2
    oth = 1 - cur
    used = i < nu_ref[0]

    def gather(tok_ref, r, slot):
        return pltpu.make_async_copy(
            h_ref.at[pl.ds(tok_ref[r], 1), :], xbuf.at[slot, pl.ds(r, 1), :], sem_g.at[slot])

    def scatter(dst_ref, r, slot):
        return pltpu.make_async_copy(
            ybuf.at[slot, pl.ds(r, 1), :], y2_ref.at[pl.ds(dst_ref[r], 1), :], sem_s.at[slot])

    def wait_gathers(slot):
        pltpu.make_async_copy(h_ref.at[pl.ds(0, tm), :], xbuf.at[slot], sem_g.at[slot]).wait()

    def wait_scatters(slot):
        pltpu.make_async_copy(ybuf.at[slot], y2_ref.at[pl.ds(0, tm), :], sem_s.at[slot]).wait()

    def issue(rr):
        r = j * per_step + rr
        gather(nxt_ref, r, oth).start()
        scatter(prv_ref, r, oth).start()

    @pl.when((i == 0) & (j == 0))
    def _():
        ybuf[...] = jnp.zeros_like(ybuf)

        def first(r, carry):
            gather(first_ref, r, 0).start()
            return carry

        lax.fori_loop(0, tm, first, 0)

    @pl.when(j == 0)
    def _():
        wait_gathers(cur)

        @pl.when(i > 0)
        def _():
            wait_scatters(cur)

        xb_sc[...] = xbuf[cur].astype(BF16)
        ybuf[cur] = jnp.zeros((tm, d), F32)

    @pl.when(used)
    def _():
        n_slots = 3 * pl.cdiv(wo_ref.shape[1], FFN_CHUNK)
        per_slot = pl.cdiv(per_step, n_slots)

        def between(k):
            for rr in range(k * per_slot, min((k + 1) * per_slot, per_step)):
                issue(rr)

        _swiglu_accumulate(xb_sc[...], wg_ref.at[0], wu_ref.at[0], wo_ref.at[0], ybuf.at[cur], between)

    @pl.when(jnp.logical_not(used))
    def _():
        def body(rr, carry):
            issue(rr)
            return carry

        lax.fori_loop(0, per_step, body, 0)

    @pl.when((i == n_i - 1) & (j == n_j - 1))
    def _():
        wait_gathers(oth)
        wait_scatters(oth)


def _expert_ffn(h2, src_tok, dst_row, tile_e, n_used, w_in_bf, w_out_bf, *, tm, tf):
    n, d = h2.shape
    n_exp, f, _ = w_out_bf.shape
    tf = min(tf, f)
    nj = f // tf
    n_tiles = src_tok.shape[0] // tm - 2

    def jeff(i, j, nu):
        return jnp.where(i < nu[0], j, nj - 1)

    smem = functools.partial(pl.BlockSpec, (tm,), memory_space=pltpu.SMEM)
    return pl.pallas_call(
        _expert_kernel,
        out_shape=jax.ShapeDtypeStruct((TOP_K * (n + tm), d), F32),
        grid_spec=pltpu.PrefetchScalarGridSpec(
            num_scalar_prefetch=2,
            grid=(n_tiles, nj),
            in_specs=[
                smem(lambda i, j, te, nu: (1,)),
                smem(lambda i, j, te, nu: (i + 2,)),
                smem(lambda i, j, te, nu: (i,)),
                pl.BlockSpec(memory_space=pl.ANY),
                pl.BlockSpec((1, d, tf), lambda i, j, te, nu: (te[i], 0, jeff(i, j, nu))),
                pl.BlockSpec((1, d, tf), lambda i, j, te, nu: (te[i], 0, jeff(i, j, nu) + nj)),
                pl.BlockSpec((1, tf, d), lambda i, j, te, nu: (te[i], jeff(i, j, nu), 0)),
            ],
            out_specs=pl.BlockSpec(memory_space=pl.ANY),
            scratch_shapes=[
                pltpu.VMEM((2, tm, d), F32),
                pltpu.VMEM((tm, d), BF16),
                pltpu.VMEM((2, tm, d), F32),
                pltpu.SemaphoreType.DMA((2,)),
                pltpu.SemaphoreType.DMA((2,)),
            ],
        ),
        compiler_params=_cparams(("arbitrary", "arbitrary")),
        name="moe_expert_ffn",
    )(tile_e, n_used, src_tok, src_tok, dst_row, h2, w_in_bf, w_in_bf, w_out_bf)


def _combine_kernel(x_ref, gate_ref, gf_ref, y2_ref, o_ref, *, final_norm):
    g = gate_ref[...]
    out = x_ref[...] + (y2_ref[0] * g[:, 0:1] + y2_ref[1] * g[:, 1:2])
    if final_norm:
        out = _rmsnorm(out, gf_ref[...])
    o_ref[...] = out


def _combine(x2, gates, y2, g_final, *, tm, final_norm):
    n, d = x2.shape
    tm = min(tm, n)
    return pl.pallas_call(
        functools.partial(_combine_kernel, final_norm=final_norm),
        out_shape=jax.ShapeDtypeStruct((n, d), F32),
        grid=(n // tm,),
        in_specs=[
            pl.BlockSpec((tm, d), lambda i: (i, 0)),
            pl.BlockSpec((tm, SUBLANES), lambda i: (i, 0)),
            pl.BlockSpec((1, d), lambda i: (0, 0)),
            pl.BlockSpec((TOP_K, tm, d), lambda i: (0, i, 0)),
        ],
        out_specs=pl.BlockSpec((tm, d), lambda i: (i, 0)),
        compiler_params=_cparams(("arbitrary",)),
        name="moe_combine",
    )(x2, gates, g_final.reshape(1, d), y2.reshape(TOP_K, -1, d))


def _moe_layer(x2, g, w_router, w_in_bf, w_out_bf, g_final, *, final_norm, tm, tm_e, tf):
    n, d = x2.shape
    n_exp = w_router.shape[1]
    tm_e = min(tm_e, n)
    h, meta, gates, cnt = _router(x2, g, w_router, tm=tm)
    counts = cnt[0, :n_exp]
    padded = (counts + tm_e - 1) // tm_e * tm_e
    pend = jnp.cumsum(padded)
    pstart = pend - padded
    dest = (pstart[meta[:, 0:TOP_K]] + meta[:, TOP_K:2 * TOP_K]).reshape(-1).astype(jnp.int32)
    cap = n * TOP_K + n_exp * tm_e
    n_tiles = cap // tm_e
    n_used = (pend[-1] // tm_e).astype(jnp.int32)
    tile_start = jnp.arange(n_tiles, dtype=jnp.int32) * tm_e
    tile_e = jnp.minimum(jnp.searchsorted(pend, tile_start, side="right"), n_exp - 1).astype(jnp.int32)
    tile_e = jnp.where(jnp.arange(n_tiles) < n_used, tile_e, tile_e[n_used - 1])
    rows = jnp.arange(cap + 2 * tm_e, dtype=jnp.int32)
    slots = jnp.full((cap + 2 * tm_e,), -1, jnp.int32).at[dest + tm_e].set(
        jnp.arange(n * TOP_K, dtype=jnp.int32), unique_indices=True, mode="promise_in_bounds")
    src_tok = jnp.maximum(slots, 0) // TOP_K
    plane = n + tm_e
    trash = (rows // tm_e % 2) * plane + n + rows % tm_e
    dst_row = jnp.where(slots < 0, trash, slots % TOP_K * plane + slots // TOP_K)

    y2 = _expert_ffn(h, src_tok, dst_row, tile_e, n_used.reshape(1), w_in_bf, w_out_bf, tm=tm_e, tf=tf)
    return _combine(x2, gates, y2, g_final, tm=tm, final_norm=final_norm)


def _trunk(x, g_mix, g_ffn, pool_w, pool_scale, g_kv, w_kv, w_q, w_o, ffn_in, ffn_out,
           w_router, moe_in, moe_out, g_final, *, ts, tm, tf_dense, tm_e, tf_moe):
    b, s, d = x.shape
    n = b * s
    depth = g_mix.shape[0]
    n_a = pool_w.shape[0]
    nh = d // HEAD_DIM
    kp = vt = None
    for l in range(depth):
        if l < n_a:
            x = _pool_layer(x, g_mix[l], pool_w[l].astype(BF16), pool_scale[l], ts=ts)
        else:
            bi = l - n_a
            qt = _q_proj(x, g_mix[l], w_q[bi].astype(BF16), tm=tm, out_scale=0.5 * HEAD_DIM ** -0.5)
            ot = _attention(qt, kp, vt)
            x = _out_proj(x, ot, w_o[bi].astype(BF16), tm=tm)
        x2 = x.reshape(n, d)
        if l % 2 == 0:
            x2 = _dense_ffn(x2, g_ffn[l], ffn_in[l // 2].astype(BF16), ffn_out[l // 2].astype(BF16),
                            tm=tm, tf=tf_dense)
        else:
            x2 = _moe_layer(x2, g_ffn[l], w_router[l // 2], moe_in[l // 2].astype(BF16),
                            moe_out[l // 2].astype(BF16), g_final,
                            final_norm=(l == depth - 1), tm=tm, tm_e=tm_e, tf=tf_moe)
        x = x2.reshape(b, s, d)
        if l == n_a - 1:
            kp, vt = _kv_proj(x2, g_kv, w_kv.astype(BF16), tm=tm)
            kp = kp.reshape(b, s // KEY_BLOCK, KEY_BLOCK, d)
            vt = vt.reshape(b, s // KEY_BLOCK, d, KEY_BLOCK)
    return x


def kernel(x, g_mix, g_ffn, pool_w, pool_scale, g_kv, w_kv, w_q, w_o, ffn_in, ffn_out,
           w_router, moe_in, moe_out, g_final):
    return _trunk(x, g_mix, g_ffn, pool_w, pool_scale, g_kv, w_kv, w_q, w_o, ffn_in, ffn_out,
                  w_router, moe_in, moe_out, g_final,
                  ts=512, tm=512, tf_dense=2816, tm_e=1024, tf_moe=1792)
```

```python
import functools

import jax
import jax.numpy as jnp
from jax import lax
from jax.experimental import pallas as pl
from jax.experimental.pallas import tpu as pltpu

RMS_EPS = 1e-6
POOL_WINDOWS = (2, 4, 8, 16)
POOL_HALO = 16
HEAD_DIM = 64
TOP_K = 2
LANES = 128
SUBLANES = 8
KEY_BLOCK = 256
KEY_CHUNK = KEY_BLOCK // SUBLANES
ATTN_PAIRS = 2
FFN_CHUNK = 256
VMEM_LIMIT = 56 * 1024 * 1024

F32 = jnp.float32
BF16 = jnp.bfloat16


def _cparams(sem):
    return pltpu.CompilerParams(dimension_semantics=sem, vmem_limit_bytes=VMEM_LIMIT)


def _rmsnorm(x, g):
    inv = lax.rsqrt(jnp.mean(x * x, axis=-1, keepdims=True) + RMS_EPS)
    return (x * inv) * g


def _dot(a, b):
    return jnp.dot(a, b, preferred_element_type=F32)


def _silu(g):
    return g * (1.0 / (1.0 + jnp.exp(-g)))


def _pool_kernel(x_ref, g_ref, w_ref, sc_ref, o_ref, l0, l1, l2, l3, *, ts, gw):
    si = pl.program_id(1)
    levels = (l0, l1, l2, l3)
    d = x_ref.shape[-1]

    @pl.when(si == 0)
    def _():
        for lv in levels:
            lv[0:POOL_HALO, :] = jnp.zeros((POOL_HALO, d), F32)

    x = x_ref[0]
    h = _rmsnorm(x, g_ref[...])
    l0[POOL_HALO:POOL_HALO + ts, :] = h
    for k in range(3):
        sh = 1 << k
        c0 = k * gw
        src, dst = levels[k], levels[k + 1]
        dst[POOL_HALO:POOL_HALO + ts, c0:] = (
            src[POOL_HALO:POOL_HALO + ts, c0:] + src[POOL_HALO - sh:POOL_HALO - sh + ts, c0:])
    c3 = 3 * gw
    s16 = l3[POOL_HALO:POOL_HALO + ts, c3:] + l3[POOL_HALO - 8:POOL_HALO - 8 + ts, c3:]

    t = si * ts + lax.broadcasted_iota(jnp.int32, (ts, 1), 0)
    for gi, w in enumerate(POOL_WINDOWS):
        sl = slice(gi * gw, (gi + 1) * gw)
        if gi < 3:
            s = levels[gi + 1][POOL_HALO:POOL_HALO + ts, sl]
        else:
            s = s16
        cnt = jnp.minimum(t + 1, w).astype(F32)
        p = (s / cnt - h[:, sl]).astype(BF16)
        y = _dot(p, w_ref[gi])
        o_ref[0, :, sl] = x[:, sl] + y * sc_ref[:, sl]

    for lv in levels:
        lv[0:POOL_HALO, :] = lv[ts:ts + POOL_HALO, :]


def _pool_layer(x, g, w_bf, scale, *, ts):
    b, s, d = x.shape
    ng, gw, _ = w_bf.shape
    ts = min(ts, s)
    kern = functools.partial(_pool_kernel, ts=ts, gw=gw)
    return pl.pallas_call(
        kern,
        out_shape=jax.ShapeDtypeStruct((b, s, d), F32),
        grid=(b, s // ts),
        in_specs=[
            pl.BlockSpec((1, ts, d), lambda bi, si: (bi, si, 0)),
            pl.BlockSpec((1, d), lambda bi, si: (0, 0)),
            pl.BlockSpec((ng, gw, gw), lambda bi, si: (0, 0, 0)),
            pl.BlockSpec((1, d), lambda bi, si: (0, 0)),
        ],
        out_specs=pl.BlockSpec((1, ts, d), lambda bi, si: (bi, si, 0)),
        scratch_shapes=[pltpu.VMEM((ts + POOL_HALO, d), F32) for _ in range(4)],
        compiler_params=_cparams(("arbitrary", "arbitrary")),
        name="pool_layer",
    )(x, g.reshape(1, d), w_bf, scale.reshape(1, d))


def _swiglu_accumulate(h, wg_ref, wu_ref, wo_ref, o_ref, between=None):
    tf = wo_ref.shape[0]
    emit = between if between is not None else (lambda k: None)
    for ci, c0 in enumerate(range(0, tf, FFN_CHUNK)):
        cs = slice(c0, min(c0 + FFN_CHUNK, tf))
        gate = _dot(h, wg_ref[:, cs])
        emit(3 * ci)
        up = _dot(h, wu_ref[:, cs])
        emit(3 * ci + 1)
        o_ref[...] += _dot((_silu(gate) * up).astype(BF16), wo_ref[cs, :])
        emit(3 * ci + 2)


def _ffn_kernel(x_ref, g_ref, wg_ref, wu_ref, wo_ref, o_ref, h_sc):
    @pl.when(pl.program_id(1) == 0)
    def _():
        h_sc[...] = _rmsnorm(x_ref[...], g_ref[...]).astype(BF16)
        o_ref[...] = x_ref[...]

    _swiglu_accumulate(h_sc[...], wg_ref, wu_ref, wo_ref, o_ref)


def _dense_ffn(x2, g, w_in_bf, w_out_bf, *, tm, tf):
    n, d = x2.shape
    f = w_out_bf.shape[0]
    tm = min(tm, n)
    tf = min(tf, f)
    nj = f // tf
    return pl.pallas_call(
        _ffn_kernel,
        out_shape=jax.ShapeDtypeStruct((n, d), F32),
        grid=(n // tm, nj),
        in_specs=[
            pl.BlockSpec((tm, d), lambda i, j: (i, 0)),
            pl.BlockSpec((1, d), lambda i, j: (0, 0)),
            pl.BlockSpec((d, tf), lambda i, j: (0, j)),
            pl.BlockSpec((d, tf), lambda i, j: (0, j + nj)),
            pl.BlockSpec((tf, d), lambda i, j: (j, 0)),
        ],
        out_specs=pl.BlockSpec((tm, d), lambda i, j: (i, 0)),
        scratch_shapes=[pltpu.VMEM((tm, d), BF16)],
        compiler_params=_cparams(("arbitrary", "arbitrary")),
        name="dense_ffn",
    )(x2, g.reshape(1, d), w_in_bf, w_in_bf, w_out_bf)


def _qproj_kernel(x_ref, g_ref, w_ref, o_ref, *, out_scale):
    h = _rmsnorm(x_ref[0], g_ref[...]).astype(BF16)
    y = _dot(h, w_ref[...]) * out_scale
    o_ref[0] = y.T.astype(BF16)


def _q_proj(x, g, w_bf, *, tm, out_scale):
    b, s, d = x.shape
    tm = min(tm, s)
    return pl.pallas_call(
        functools.partial(_qproj_kernel, out_scale=out_scale),
        out_shape=jax.ShapeDtypeStruct((b, d, s), BF16),
        grid=(b, s // tm),
        in_specs=[
            pl.BlockSpec((1, tm, d), lambda bi, si: (bi, si, 0)),
            pl.BlockSpec((1, d), lambda bi, si: (0, 0)),
            pl.BlockSpec((d, d), lambda bi, si: (0, 0)),
        ],
        out_specs=pl.BlockSpec((1, d, tm), lambda bi, si: (bi, 0, si)),
        compiler_params=_cparams(("arbitrary", "arbitrary")),
        name="q_proj",
    )(x, g.reshape(1, d), w_bf)


def _kvproj_kernel(x_ref, g_ref, w_ref, k_ref, vt_ref):
    tm, d = x_ref.shape
    h = _rmsnorm(x_ref[...], g_ref[...]).astype(BF16)
    y = _dot(h, w_ref[...]).astype(BF16)
    row = lax.broadcasted_iota(jnp.int32, (KEY_BLOCK, KEY_BLOCK), 0)
    col = lax.broadcasted_iota(jnp.int32, (KEY_BLOCK, KEY_BLOCK), 1)
    perm = (col == KEY_CHUNK * (row % SUBLANES) + row // SUBLANES).astype(BF16)
    for blk in range(tm // KEY_BLOCK):
        rows = slice(blk * KEY_BLOCK, (blk + 1) * KEY_BLOCK)
        p = _dot(perm, y[rows, :])
        k_ref[rows, :] = p[:, :d].astype(BF16)
        vt_ref[blk] = p[:, d:].T.astype(BF16)


def _kv_proj(x2, g, w_bf, *, tm):
    n, d = x2.shape
    tm = max(min(tm, n), KEY_BLOCK)
    nb = tm // KEY_BLOCK
    return pl.pallas_call(
        _kvproj_kernel,
        out_shape=(jax.ShapeDtypeStruct((n, d), BF16),
                   jax.ShapeDtypeStruct((n // KEY_BLOCK, d, KEY_BLOCK), BF16)),
        grid=(n // tm,),
        in_specs=[
            pl.BlockSpec((tm, d), lambda i: (i, 0)),
            pl.BlockSpec((1, d), lambda i: (0, 0)),
            pl.BlockSpec((d, 2 * d), lambda i: (0, 0)),
        ],
        out_specs=(pl.BlockSpec((tm, d), lambda i: (i, 0)),
                   pl.BlockSpec((nb, d, KEY_BLOCK), lambda i: (i, 0, 0))),
        compiler_params=_cparams(("arbitrary",)),
        name="kv_proj",
    )(x2, g.reshape(1, d), w_bf)


def _oproj_kernel(x_ref, ot_ref, w_ref, o_ref):
    o = ot_ref[0].astype(F32).T.astype(BF16)
    o_ref[0] = x_ref[0] + _dot(o, w_ref[...])


def _out_proj(x, ot_bf, w_bf, *, tm):
    b, s, d = x.shape
    tm = min(tm, s)
    return pl.pallas_call(
        _oproj_kernel,
        out_shape=jax.ShapeDtypeStruct((b, s, d), F32),
        grid=(b, s // tm),
        in_specs=[
            pl.BlockSpec((1, tm, d), lambda bi, si: (bi, si, 0)),
            pl.BlockSpec((1, d, tm), lambda bi, si: (bi, 0, si)),
            pl.BlockSpec((d, d), lambda bi, si: (0, 0)),
        ],
        out_specs=pl.BlockSpec((1, tm, d), lambda bi, si: (bi, si, 0)),
        compiler_params=_cparams(("arbitrary", "arbitrary")),
        name="out_proj",
    )(x, ot_bf, w_bf)


def _attn_kernel(qt_ref, kp_ref, vt_ref, o_ref, z_sc, zd_sc, d_sc, at_sc, acc_sc):
    kb = KEY_BLOCK
    tq = KEY_BLOCK
    w = 2 * tq
    nr = kb // SUBLANES
    pw = 2 * HEAD_DIM
    n_pairs = qt_ref.shape[1] // pw
    s = qt_ref.shape[-1]
    sub = lax.broadcasted_iota(jnp.int32, (SUBLANES, w), 0)
    q_off = lax.broadcasted_iota(jnp.int32, (SUBLANES, w), 1) % tq
    head_a = lax.broadcasted_iota(jnp.int32, (pw, tq), 0) < HEAD_DIM

    def step(j, cs, qts, qi, diag, qts_next=None):
        slot = j % 2
        if diag:
            acc_sc[...] = jnp.zeros_like(acc_sc)
        else:
            for p in range(n_pairs):
                acc_sc[p] += _dot(vt_ref[0, j + 1, p * pw:(p + 1) * pw, :], at_sc[p])
        runs = []
        for p in range(n_pairs):
            zs = zd_sc.at[p] if diag else z_sc.at[slot, p]
            run = jnp.ones((SUBLANES, w), F32)
            for r in reversed(range(nr)):
                sl = slice(r * SUBLANES, (r + 1) * SUBLANES)
                t = jnp.tanh(zs[sl, :]) * 0.5
                if diag:
                    t = jnp.where((r + KEY_CHUNK * sub) < q_off, t, -0.5)
                d_sc[p, sl, :] = (0.5 + t) * run
                run = run * (0.5 - t)
            runs.append(run)
        if not diag or qi > 0:
            jk = jnp.maximum(j - 1, 0)
            for p in range(n_pairs):
                z_sc[1 - slot, p] = _dot(kp_ref[0, jk, :, p * pw:(p + 1) * pw], qts[p])
        if diag and qts_next is not None:
            for p in range(n_pairs):
                zd_sc[p] = _dot(kp_ref[0, qi + 1, :, p * pw:(p + 1) * pw], qts_next[p])
        out = []
        for p in range(n_pairs):
            run = runs[p]
            y = jnp.where(sub + 1 < SUBLANES, pltpu.roll(run, SUBLANES - 1, axis=0), 1.0)
            for dd in (1, 2, 4):
                y = y * jnp.where(sub + dd < SUBLANES, pltpu.roll(y, SUBLANES - dd, axis=0), 1.0)
            oc = y * cs[p]
            for r in range(0, nr, 2):
                two = [d_sc[p, rr * SUBLANES:(rr + 1) * SUBLANES, :] * oc for rr in (r, r + 1)]
                at_sc[p, r * SUBLANES:(r + 2) * SUBLANES, :] = jnp.concatenate(two, axis=0).astype(BF16)
            out.append(cs[p] * (y[0:1, :] * run[0:1, :]))
        return tuple(out)

    def q_operands(qi):
        out = []
        for p in range(n_pairs):
            q2 = qt_ref[0, p * pw:(p + 1) * pw, qi * tq:(qi + 1) * tq]
            zero = jnp.zeros_like(q2)
            out.append(jnp.concatenate([jnp.where(head_a, q2, zero), jnp.where(head_a, zero, q2)], axis=1))
        return out

    n_q = s // tq
    qts = q_operands(0)
    for p in range(n_pairs):
        zd_sc[p] = _dot(kp_ref[0, 0, :, p * pw:(p + 1) * pw], qts[p])
    for qi in range(n_q):
        cols = slice(qi * tq, (qi + 1) * tq)
        qts_next = q_operands(qi + 1) if qi + 1 < n_q else None
        ones = tuple(jnp.ones((1, w), F32) for _ in range(n_pairs))
        cs = step(qi, ones, qts, qi, True, qts_next)
        if qi > 0:
            cs = lax.fori_loop(0, qi, lambda it, cc: step(qi - 1 - it, cc, qts, qi, False), cs)
        qts = qts_next
        for p in range(n_pairs):
            acc = acc_sc[p] + _dot(vt_ref[0, 0, p * pw:(p + 1) * pw, :], at_sc[p])
            o_ref[0, p * pw:p * pw + HEAD_DIM, cols] = acc[:HEAD_DIM, :tq].astype(BF16)
            o_ref[0, p * pw + HEAD_DIM:(p + 1) * pw, cols] = acc[HEAD_DIM:, tq:].astype(BF16)


def _attention(qt, kp, vt):
    b, d, s = qt.shape
    nkb = s // KEY_BLOCK
    gw = ATTN_PAIRS * 2 * HEAD_DIM
    w = 2 * KEY_BLOCK
    return pl.pallas_call(
        _attn_kernel,
        out_shape=jax.ShapeDtypeStruct((b, d, s), BF16),
        grid=(b, d // gw),
        in_specs=[
            pl.BlockSpec((1, gw, s), lambda bi, gi: (bi, gi, 0)),
            pl.BlockSpec((1, nkb, KEY_BLOCK, gw), lambda bi, gi: (bi, 0, 0, gi)),
            pl.BlockSpec((1, nkb, gw, KEY_BLOCK), lambda bi, gi: (bi, 0, gi, 0)),
        ],
        out_specs=pl.BlockSpec((1, gw, s), lambda bi, gi: (bi, gi, 0)),
        scratch_shapes=[
            pltpu.VMEM((2, ATTN_PAIRS, KEY_BLOCK, w), F32),
            pltpu.VMEM((ATTN_PAIRS, KEY_BLOCK, w), F32),
            pltpu.VMEM((ATTN_PAIRS, KEY_BLOCK, w), F32),
            pltpu.VMEM((ATTN_PAIRS, KEY_BLOCK, w), BF16),
            pltpu.VMEM((ATTN_PAIRS, 2 * HEAD_DIM, w), F32),
        ],
        compiler_params=_cparams(("arbitrary", "arbitrary")),
        name="stickbreak_attn",
    )(qt, kp, vt)


def _router_kernel(x_ref, g_ref, wr_ref, h_ref, meta_ref, gate_ref, cnt_ref, run_sc, *, n_exp):
    i = pl.program_id(0)
    tm = x_ref.shape[0]

    @pl.when(i == 0)
    def _():
        run_sc[...] = jnp.zeros_like(run_sc)

    h = _rmsnorm(x_ref[...], g_ref[...])
    h_ref[...] = h
    w = wr_ref[...]
    h_hi = h.astype(BF16)
    h_lo = (h - h_hi.astype(F32)).astype(BF16)
    w_hi = w.astype(BF16)
    w_lo = (w - w_hi.astype(F32)).astype(BF16)
    logits = (_dot(h_hi, w_hi) + _dot(h_hi, w_lo)) + _dot(h_lo, w_hi)
    lane = lax.broadcasted_iota(jnp.int32, (tm, LANES), 1)
    neg = jnp.float32(-jnp.inf)
    logits = jnp.where(lane < n_exp, logits, neg)
    m1 = jnp.max(logits, axis=-1, keepdims=True)
    i1 = jnp.min(jnp.where(logits == m1, lane, LANES), axis=-1, keepdims=True)
    rest = jnp.where(lane == i1, neg, logits)
    m2 = jnp.max(rest, axis=-1, keepdims=True)
    i2 = jnp.min(jnp.where(rest == m2, lane, LANES), axis=-1, keepdims=True)
    e = jnp.exp(m2 - m1)
    den = 1.0 + e
    g1 = 1.0 / den
    g2 = e / den

    sel1 = lane == i1
    sel2 = lane == i2
    onehot = jnp.where(sel1 | sel2, 1.0, 0.0)
    tri = (lax.broadcasted_iota(jnp.int32, (tm, tm), 1)
           < lax.broadcasted_iota(jnp.int32, (tm, tm), 0)).astype(BF16)
    before = _dot(tri, onehot.astype(BF16)) + run_sc[...]
    r1 = jnp.sum(jnp.where(sel1, before, 0.0), axis=-1, keepdims=True).astype(jnp.int32)
    r2 = jnp.sum(jnp.where(sel2, before, 0.0), axis=-1, keepdims=True).astype(jnp.int32)
    run_sc[...] += jnp.sum(onehot, axis=0, keepdims=True)

    meta = jnp.where(lane == 0, i1, jnp.where(lane == 1, i2, jnp.where(lane == 2, r1, jnp.where(lane == 3, r2, 0))))
    gates = jnp.where(lane == 0, g1, jnp.where(lane == 1, g2, 0.0))
    meta_ref[...] = meta[:, :SUBLANES]
    gate_ref[...] = gates[:, :SUBLANES]
    cnt_ref[...] = run_sc[...].astype(jnp.int32)


def _router(x2, g, w_router, *, tm):
    n, d = x2.shape
    n_exp = w_router.shape[1]
    tm = min(tm, n)
    wr = jnp.zeros((d, LANES), F32).at[:, :n_exp].set(w_router)
    return pl.pallas_call(
        functools.partial(_router_kernel, n_exp=n_exp),
        out_shape=(
            jax.ShapeDtypeStruct((n, d), F32),
            jax.ShapeDtypeStruct((n, SUBLANES), jnp.int32),
            jax.ShapeDtypeStruct((n, SUBLANES), F32),
            jax.ShapeDtypeStruct((1, LANES), jnp.int32),
        ),
        grid=(n // tm,),
        in_specs=[
            pl.BlockSpec((tm, d), lambda i: (i, 0)),
            pl.BlockSpec((1, d), lambda i: (0, 0)),
            pl.BlockSpec((d, LANES), lambda i: (0, 0)),
        ],
        out_specs=(
            pl.BlockSpec((tm, d), lambda i: (i, 0)),
            pl.BlockSpec((tm, SUBLANES), lambda i: (i, 0)),
            pl.BlockSpec((tm, SUBLANES), lambda i: (i, 0)),
            pl.BlockSpec((1, LANES), lambda i: (0, 0)),
        ),
        scratch_shapes=[pltpu.VMEM((1, LANES), F32)],
        compiler_params=_cparams(("arbitrary",)),
        name="moe_router",
    )(x2, g.reshape(1, d), wr)


def _expert_kernel(te_ref, nu_ref, first_ref, nxt_ref, prv_ref, h_ref, wg_ref, wu_ref, wo_ref, y2_ref,
                   xbuf, xb_sc, ybuf, sem_g, sem_s):
    i = pl.program_id(0)
    j = pl.program_id(1)
    n_i = pl.num_programs(0)
    n_j = pl.num_programs(1)
    tm, d = xb_sc.shape
    per_step = tm // n_j
    cur = i % 2
    oth = 1 - cur
    used = i < nu_ref[0]

    def gather(tok_ref, r, slot):
        return pltpu.make_async_copy(
            h_ref.at[pl.ds(tok_ref[r], 1), :], xbuf.at[slot, pl.ds(r, 1), :], sem_g.at[slot])

    def scatter(dst_ref, r, slot):
        return pltpu.make_async_copy(
            ybuf.at[slot, pl.ds(r, 1), :], y2_ref.at[pl.ds(dst_ref[r], 1), :], sem_s.at[slot])

    def wait_gathers(slot):
        pltpu.make_async_copy(h_ref.at[pl.ds(0, tm), :], xbuf.at[slot], sem_g.at[slot]).wait()

    def wait_scatters(slot):
        pltpu.make_async_copy(ybuf.at[slot], y2_ref.at[pl.ds(0, tm), :], sem_s.at[slot]).wait()

    def issue(rr):
        r = j * per_step + rr
        gather(nxt_ref, r, oth).start()
        scatter(prv_ref, r, oth).start()

    @pl.when((i == 0) & (j == 0))
    def _():
        ybuf[...] = jnp.zeros_like(ybuf)

        def first(r, carry):
            gather(first_ref, r, 0).start()
            return carry

        lax.fori_loop(0, tm, first, 0)

    @pl.when(j == 0)
    def _():
        wait_gathers(cur)

        @pl.when(i > 0)
        def _():
            wait_scatters(cur)

        xb_sc[...] = xbuf[cur].astype(BF16)
        ybuf[cur] = jnp.zeros((tm, d), F32)

    @pl.when(used)
    def _():
        n_slots = 3 * pl.cdiv(wo_ref.shape[1], FFN_CHUNK)
        per_slot = pl.cdiv(per_step, n_slots)

        def between(k):
            for rr in range(k * per_slot, min((k + 1) * per_slot, per_step)):
                issue(rr)

        _swiglu_accumulate(xb_sc[...], wg_ref.at[0], wu_ref.at[0], wo_ref.at[0], ybuf.at[cur], between)

    @pl.when(jnp.logical_not(used))
    def _():
        def body(rr, carry):
            issue(rr)
            return carry

        lax.fori_loop(0, per_step, body, 0)

    @pl.when((i == n_i - 1) & (j == n_j - 1))
    def _():
        wait_gathers(oth)
        wait_scatters(oth)


def _expert_ffn(h2, src_tok, dst_row, tile_e, n_used, w_in_bf, w_out_bf, *, tm, tf):
    n, d = h2.shape
    n_exp, f, _ = w_out_bf.shape
    tf = min(tf, f)
    nj = f // tf
    n_tiles = src_tok.shape[0] // tm - 2

    def jeff(i, j, nu):
        return jnp.where(i < nu[0], j, nj - 1)

    smem = functools.partial(pl.BlockSpec, (tm,), memory_space=pltpu.SMEM)
    w_mode = pl.Buffered(1) if nj == 1 else None
    return pl.pallas_call(
        _expert_kernel,
        out_shape=jax.ShapeDtypeStruct((TOP_K * (n + tm), d), F32),
        grid_spec=pltpu.PrefetchScalarGridSpec(
            num_scalar_prefetch=2,
            grid=(n_tiles, nj),
            in_specs=[
                smem(lambda i, j, te, nu: (1,)),
                smem(lambda i, j, te, nu: (i + 2,)),
                smem(lambda i, j, te, nu: (i,)),
                pl.BlockSpec(memory_space=pl.ANY),
                pl.BlockSpec((1, d, tf), lambda i, j, te, nu: (te[i], 0, jeff(i, j, nu)), pipeline_mode=w_mode),
                pl.BlockSpec((1, d, tf), lambda i, j, te, nu: (te[i], 0, jeff(i, j, nu) + nj), pipeline_mode=w_mode),
                pl.BlockSpec((1, tf, d), lambda i, j, te, nu: (te[i], jeff(i, j, nu), 0), pipeline_mode=w_mode),
            ],
            out_specs=pl.BlockSpec(memory_space=pl.ANY),
            scratch_shapes=[
                pltpu.VMEM((2, tm, d), F32),
                pltpu.VMEM((tm, d), BF16),
                pltpu.VMEM((2, tm, d), F32),
                pltpu.SemaphoreType.DMA((2,)),
                pltpu.SemaphoreType.DMA((2,)),
            ],
        ),
        compiler_params=_cparams(("arbitrary", "arbitrary")),
        name="moe_expert_ffn",
    )(tile_e, n_used, src_tok, src_tok, dst_row, h2, w_in_bf, w_in_bf, w_out_bf)


def _combine_kernel(x_ref, gate_ref, gf_ref, y2_ref, o_ref, *, final_norm):
    g = gate_ref[...]
    out = x_ref[...] + (y2_ref[0] * g[:, 0:1] + y2_ref[1] * g[:, 1:2])
    if final_norm:
        out = _rmsnorm(out, gf_ref[...])
    o_ref[...] = out


def _combine(x2, gates, y2, g_final, *, tm, final_norm):
    n, d = x2.shape
    tm = min(tm, n)
    return pl.pallas_call(
        functools.partial(_combine_kernel, final_norm=final_norm),
        out_shape=jax.ShapeDtypeStruct((n, d), F32),
        grid=(n // tm,),
        in_specs=[
            pl.BlockSpec((tm, d), lambda i: (i, 0)),
            pl.BlockSpec((tm, SUBLANES), lambda i: (i, 0)),
            pl.BlockSpec((1, d), lambda i: (0, 0)),
            pl.BlockSpec((TOP_K, tm, d), lambda i: (0, i, 0)),
        ],
        out_specs=pl.BlockSpec((tm, d), lambda i: (i, 0)),
        compiler_params=_cparams(("arbitrary",)),
        name="moe_combine",
    )(x2, gates, g_final.reshape(1, d), y2.reshape(TOP_K, -1, d))


def _moe_layer(x2, g, w_router, w_in_bf, w_out_bf, g_final, *, final_norm, tm, tm_e, tf):
    n, d = x2.shape
    n_exp = w_router.shape[1]
    tm_e = min(tm_e, n)
    h, meta, gates, cnt = _router(x2, g, w_router, tm=tm)
    counts = cnt[0, :n_exp]
    padded = (counts + tm_e - 1) // tm_e * tm_e
    pend = jnp.cumsum(padded)
    pstart = pend - padded
    dest = (pstart[meta[:, 0:TOP_K]] + meta[:, TOP_K:2 * TOP_K]).reshape(-1).astype(jnp.int32)
    cap = n * TOP_K + n_exp * tm_e
    n_tiles = cap // tm_e
    n_used = (pend[-1] // tm_e).astype(jnp.int32)
    tile_start = jnp.arange(n_tiles, dtype=jnp.int32) * tm_e
    tile_e = jnp.minimum(jnp.searchsorted(pend, tile_start, side="right"), n_exp - 1).astype(jnp.int32)
    tile_e = jnp.where(jnp.arange(n_tiles) < n_used, tile_e, tile_e[n_used - 1])
    rows = jnp.arange(cap + 2 * tm_e, dtype=jnp.int32)
    slots = jnp.full((cap + 2 * tm_e,), -1, jnp.int32).at[dest + tm_e].set(
        jnp.arange(n * TOP_K, dtype=jnp.int32), unique_indices=True, mode="promise_in_bounds")
    src_tok = jnp.maximum(slots, 0) // TOP_K
    plane = n + tm_e
    trash = (rows // tm_e % 2) * plane + n + rows % tm_e
    dst_row = jnp.where(slots < 0, trash, slots % TOP_K * plane + slots // TOP_K)

    y2 = _expert_ffn(h, src_tok, dst_row, tile_e, n_used.reshape(1), w_in_bf, w_out_bf, tm=tm_e, tf=tf)
    return _combine(x2, gates, y2, g_final, tm=tm, final_norm=final_norm)


def _trunk(x, g_mix, g_ffn, pool_w, pool_scale, g_kv, w_kv, w_q, w_o, ffn_in, ffn_out,
           w_router, moe_in, moe_out, g_final, *, ts, tm, tf_dense, tm_e, tf_moe):
    b, s, d = x.shape
    n = b * s
    depth = g_mix.shape[0]
    n_a = pool_w.shape[0]
    nh = d // HEAD_DIM
    kp = vt = None
    for l in range(depth):
        if l < n_a:
            x = _pool_layer(x, g_mix[l], pool_w[l].astype(BF16), pool_scale[l], ts=ts)
        else:
            bi = l - n_a
            qt = _q_proj(x, g_mix[l], w_q[bi].astype(BF16), tm=tm, out_scale=0.5 * HEAD_DIM ** -0.5)
            ot = _attention(qt, kp, vt)
            x = _out_proj(x, ot, w_o[bi].astype(BF16), tm=tm)
        x2 = x.reshape(n, d)
        if l % 2 == 0:
            x2 = _dense_ffn(x2, g_ffn[l], ffn_in[l // 2].astype(BF16), ffn_out[l // 2].astype(BF16),
                            tm=tm, tf=tf_dense)
        else:
            x2 = _moe_layer(x2, g_ffn[l], w_router[l // 2], moe_in[l // 2].astype(BF16),
                            moe_out[l // 2].astype(BF16), g_final,
                            final_norm=(l == depth - 1), tm=tm, tm_e=tm_e, tf=tf_moe)
        x = x2.reshape(b, s, d)
        if l == n_a - 1:
            kp, vt = _kv_proj(x2, g_kv, w_kv.astype(BF16), tm=tm)
            kp = kp.reshape(b, s // KEY_BLOCK, KEY_BLOCK, d)
            vt = vt.reshape(b, s // KEY_BLOCK, d, KEY_BLOCK)
    return x


def kernel(x, g_mix, g_ffn, pool_w, pool_scale, g_kv, w_kv, w_q, w_o, ffn_in, ffn_out,
           w_router, moe_in, moe_out, g_final):
    return _trunk(x, g_mix, g_ffn, pool_w, pool_scale, g_kv, w_kv, w_q, w_o, ffn_in, ffn_out,
                  w_router, moe_in, moe_out, g_final,
                  ts=512, tm=512, tf_dense=2816, tm_e=1024, tf_moe=3584)
```
